```python
import math
import jax
import jax.numpy as jnp
from jax import lax
import numpy as np

D_MODEL = 2048
BATCH = 4
SEQ = 4096
DEPTH = 2
DEC_BATCH = 32
DEC_SEQ = 16
PAST_LEN = 2048

CHUNK = 64
N_EVEN = (DEPTH + 1) // 2
N_ODD = DEPTH // 2
ALPHA = (2 * DEPTH) ** 0.25
BETA_INIT = (8 * DEPTH) ** -0.25
LN_EPS = 1e-5
L2_EPS = 1e-6

GLA_HEADS = 4
GLA_DK = 128
GLA_DV = 256
GLA_QK = GLA_HEADS * GLA_DK
GLA_V = GLA_HEADS * GLA_DV
GLA_GATE_RANK = 16
GLA_GATE_NORM = 16.0
CONV_CH = 1024
CONV_W = 31
AB_SPLITS = (GLA_QK, 2 * GLA_QK, 2 * GLA_QK + GLA_V, 2 * GLA_QK + 2 * GLA_V,
             2 * GLA_QK + 2 * GLA_V + GLA_GATE_RANK)
AB_IN = AB_SPLITS[-1] + 2 * CONV_CH
AB_MIX = GLA_V + CONV_CH
GDN_QK_HEADS = 16
GDN_V_HEADS = 32
GDN_DK = 128
GDN_DV = 128
GDN_CONV_W = 4
GDN_Q = GDN_QK_HEADS * GDN_DK
GDN_V = GDN_V_HEADS * GDN_DV
GDN_CONV_DIM = 2 * GDN_Q + GDN_V
C_SPLITS = (GDN_CONV_DIM, GDN_CONV_DIM + GDN_V, GDN_CONV_DIM + GDN_V + GDN_V_HEADS)
C_IN = C_SPLITS[-1] + GDN_V_HEADS
MEM_LEN = 256
MEM_HEADS = 4
MEM_HD = D_MODEL // MEM_HEADS
FF_DENSE = 5632
N_EXPERTS = 8
TOP_K = 2
FF_EXPERT = 7168

kernel_name = 'hybrid_gla_conformer_gdn_stream_step'

F32 = jnp.float32


def _layer_norm(x, g, b):
    xf = x.astype(F32)
    mu = jnp.mean(xf, -1, keepdims=True)
    xc = xf - mu
    var = jnp.mean(xc * xc, -1, keepdims=True)
    return (xc * lax.rsqrt(var + LN_EPS) * g.astype(F32) + b.astype(F32)).astype(x.dtype)


def _rms_norm(x, g):
    xf = x.astype(F32)
    return (xf * lax.rsqrt(jnp.mean(xf * xf, -1, keepdims=True) + LN_EPS) * g.astype(F32)).astype(x.dtype)


def _l2norm(x):
    xf = x.astype(F32)
    return xf * lax.rsqrt(jnp.sum(xf * xf, -1, keepdims=True) + L2_EPS)


def _causal_dwconv(u, buf, w):
    ext = jnp.concatenate([buf.astype(u.dtype), u], axis=1)
    width, ch = w.shape
    out = lax.conv_general_dilated(ext, w.astype(u.dtype)[:, None, :], window_strides=(1,), padding='VALID',
                                   dimension_numbers=('NWC', 'WIO', 'NWC'), feature_group_count=ch)
    return out, ext[:, ext.shape[1] - (width - 1):]


def _to_chunks(t, size):
    b, h, n = t.shape[:3]
    return jnp.moveaxis(t.reshape(b, h, n // size, size, *t.shape[3:]), 2, 0)


def _from_chunks(t):
    n, b, h, size = t.shape[:4]
    return jnp.moveaxis(t, 0, 2).reshape(b, h, n * size, *t.shape[4:])


def _gla_recurrence(q, k, v, log_a, s0):
    size = min(CHUNK, q.shape[2])
    incl = jnp.tril(jnp.ones((size, size), dtype=bool))

    def step(s, inp):
        qc, kc, vc, ac = inp
        bcum = jnp.cumsum(ac, axis=2)
        o_inter = jnp.einsum('bhtc,bhcv->bhtv', qc * jnp.exp(bcum), s)
        rel = jnp.exp(jnp.where(incl[:, :, None], bcum[:, :, :, None, :] - bcum[:, :, None, :, :], -jnp.inf))
        att = jnp.einsum('bhtsc,bhsc->bhts', qc[:, :, :, None, :] * rel, kc)
        o = o_inter + jnp.einsum('bhts,bhsv->bhtv', att, vc)
        b_last = bcum[:, :, -1]
        s = s * jnp.exp(b_last)[..., None] + jnp.einsum(
            'bhsc,bhsv->bhcv', kc * jnp.exp(b_last[:, :, None] - bcum), vc)
        return s, o

    s, o = lax.scan(step, s0, (_to_chunks(q, size), _to_chunks(k, size), _to_chunks(v, size),
                               _to_chunks(log_a, size)))
    return _from_chunks(o), s


def _gdn_recurrence(q, k, v, g, beta, s0):
    size = min(CHUNK, q.shape[2])
    incl = jnp.tril(jnp.ones((size, size), dtype=bool))
    strict = jnp.tril(jnp.ones((size, size), dtype=bool), -1)
    eye = jnp.eye(size, dtype=F32)

    def step(s, inp):
        qc, kc, vc, gc, bc = inp
        gcum = jnp.cumsum(gc, axis=-1)
        rel = jnp.exp(jnp.where(incl, gcum[..., :, None] - gcum[..., None, :], -jnp.inf))
        kb = kc * bc[..., None]
        m = jnp.where(strict, jnp.einsum('bhtc,bhsc->bhts', kb, kc) * rel, 0.0)
        tinv = lax.linalg.triangular_solve(eye + m, jnp.broadcast_to(eye, m.shape), left_side=True,
                                           lower=True, unit_diagonal=True)
        u = jnp.einsum('bhts,bhsv->bhtv', tinv, vc * bc[..., None])
        w = jnp.einsum('bhts,bhsc->bhtc', tinv, kb * jnp.exp(gcum)[..., None])
        v_new = u - jnp.einsum('bhtc,bhcv->bhtv', w, s)
        att = jnp.einsum('bhtc,bhsc->bhts', qc, kc) * rel
        o = (jnp.einsum('bhtc,bhcv->bhtv', qc * jnp.exp(gcum)[..., None], s)
             + jnp.einsum('bhts,bhsv->bhtv', att, v_new))
        g_last = gcum[..., -1]
        s = s * jnp.exp(g_last)[..., None, None] + jnp.einsum(
            'bhsc,bhsv->bhcv', kc * jnp.exp(g_last[..., None] - gcum)[..., None], v_new)
        return s, o

    s, o = lax.scan(step, s0, (_to_chunks(q, size), _to_chunks(k, size), _to_chunks(v, size),
                               _to_chunks(g, size), _to_chunks(beta, size)))
    return _from_chunks(o), s


def _mixer_ab(x, s_gla, conv_buf, w_in, w_gk_up, b_gk, gla_norm_g, dw_w, dw_b, cln_g, cln_b, w_out):
    bsz, t, _ = x.shape
    proj = x @ w_in
    q, k, v, r, gl, glu = jnp.split(proj, AB_SPLITS, axis=-1)

    def heads(z, d):
        return z.reshape(bsz, t, GLA_HEADS, d).transpose(0, 2, 1, 3).astype(F32)

    log_a = jax.nn.log_sigmoid((gl @ w_gk_up + b_gk).astype(F32)) / GLA_GATE_NORM
    o, s_new = _gla_recurrence(heads(q, GLA_DK) * GLA_DK ** -0.5, heads(k, GLA_DK), heads(v, GLA_DV),
                               heads(log_a, GLA_DK), s_gla.astype(F32))
    o = _rms_norm(o.transpose(0, 2, 1, 3), gla_norm_g) * jax.nn.silu(r.astype(F32)).reshape(
        bsz, t, GLA_HEADS, GLA_DV)
    o_a = o.reshape(bsz, t, GLA_V).astype(x.dtype)
    a, gate = jnp.split(glu, 2, axis=-1)
    u = a * jax.nn.sigmoid(gate)
    c, buf_new = _causal_dwconv(u, conv_buf, dw_w)
    c = jax.nn.silu(_layer_norm(c + dw_b, cln_g, cln_b))
    y = jnp.concatenate([o_a, c.astype(x.dtype)], axis=-1) @ w_out
    return y, s_new.astype(s_gla.dtype), buf_new.astype(conv_buf.dtype)


def _mixer_c(x, s_gdn, conv_buf, w_in, conv_w, a_log, dt_bias, norm_g, w_out):
    bsz, t, _ = x.shape
    proj = x @ w_in
    qkv_in, z, b, a = jnp.split(proj, C_SPLITS, axis=-1)
    qkv, buf_new = _causal_dwconv(qkv_in, conv_buf, conv_w)
    qkv = jax.nn.silu(qkv.astype(F32))
    q, k, v = jnp.split(qkv, (GDN_Q, 2 * GDN_Q), axis=-1)
    rep = GDN_V_HEADS // GDN_QK_HEADS
    q = jnp.repeat(_l2norm(q.reshape(bsz, t, GDN_QK_HEADS, GDN_DK)), rep, axis=2) * GDN_DK ** -0.5
    k = jnp.repeat(_l2norm(k.reshape(bsz, t, GDN_QK_HEADS, GDN_DK)), rep, axis=2)
    v = v.reshape(bsz, t, GDN_V_HEADS, GDN_DV)
    beta = jax.nn.sigmoid(b.astype(F32))
    g = -jnp.exp(a_log.astype(F32)) * jax.nn.softplus(a.astype(F32) + dt_bias.astype(F32))

    def swap(zz):
        return jnp.moveaxis(zz, 2, 1)

    o, s_new = _gdn_recurrence(swap(q), swap(k), swap(v), swap(g), swap(beta), s_gdn.astype(F32))
    o = _rms_norm(swap(o), norm_g) * jax.nn.silu(z.astype(F32)).reshape(bsz, t, GDN_V_HEADS, GDN_DV)
    y = o.reshape(bsz, t, GDN_V).astype(x.dtype) @ w_out
    return y, s_new.astype(s_gdn.dtype), buf_new.astype(conv_buf.dtype)


def _memory_kv(mem, w_k, w_v):
    bsz = mem.shape[0]
    k = jnp.einsum('bmd,lde->lbme', mem, w_k).reshape(DEPTH, bsz, MEM_LEN, MEM_HEADS, MEM_HD)
    v = jnp.einsum('bmd,lde->lbme', mem, w_v).reshape(DEPTH, bsz, MEM_LEN, MEM_HEADS, MEM_HD)
    return k, v


def _memory_attention(x, mk, mv, w_q, w_o):
    bsz, t, _ = x.shape
    q = (x @ w_q).reshape(bsz, t, MEM_HEADS, MEM_HD)
    s = jnp.einsum('bthd,bmhd->bhtm', q, mk.astype(x.dtype)).astype(F32) * MEM_HD ** -0.5
    p = jax.nn.softmax(s, axis=-1).astype(x.dtype)
    o = jnp.einsum('bhtm,bmhd->bthd', p, mv.astype(x.dtype)).reshape(bsz, t, D_MODEL)
    return o @ w_o


def _swiglu(x, w_gate, w_up, w_down):
    return (jax.nn.silu(x @ w_gate) * (x @ w_up)) @ w_down


def _moe_swiglu(x, router_w, router_b, w_gate, w_up, w_down):
    logits = (x @ router_w).astype(F32) + router_b.astype(F32)
    top_val, top_idx = lax.top_k(logits, TOP_K)
    top_w = jax.nn.softmax(top_val, axis=-1)
    gate = jnp.einsum('btk,btke->bte', top_w, jax.nn.one_hot(top_idx, N_EXPERTS, dtype=F32)).astype(x.dtype)
    y = jnp.zeros_like(x)
    for e in range(N_EXPERTS):
        y = y + _swiglu(x, w_gate[e], w_up[e], w_down[e]) * gate[..., e:e + 1]
    return y


def _trunk(x, gla_s, conf_buf, gdn_s, gdn_buf, mem_k, mem_v, p):
    gla_new, conf_new, gdn_new, gdnc_new = [], [], [], []
    for layer in range(DEPTH):
        i = layer // 2
        if layer % 2 == 0:
            mix, s, b = _mixer_ab(x, gla_s[i], conf_buf[i], p['w_in_ab'][i], p['w_gk_up'][i], p['b_gk'][i],
                                  p['gla_norm_g'][i], p['conv_dw_w'][i], p['conv_dw_b'][i], p['conv_ln_g'][i],
                                  p['conv_ln_b'][i], p['w_out_ab'][i])
            gla_new.append(s)
            conf_new.append(b)
        else:
            mix, s, b = _mixer_c(x, gdn_s[i], gdn_buf[i], p['w_in_c'][i], p['gdn_conv_w'][i], p['gdn_a_log'][i],
                                 p['gdn_dt_bias'][i], p['gdn_norm_g'][i], p['w_out_c'][i])
            gdn_new.append(s)
            gdnc_new.append(b)
        x = _layer_norm(ALPHA * x + mix, p['ln_mix_g'][layer], p['ln_mix_b'][layer])
        x = _layer_norm(ALPHA * x + _memory_attention(x, mem_k[layer], mem_v[layer], p['w_mq'][layer],
                                                      p['w_mo'][layer]),
                        p['ln_mem_g'][layer], p['ln_mem_b'][layer])
        if layer % 2 == 0:
            ff = _swiglu(x, p['ff_w_gate'][i], p['ff_w_up'][i], p['ff_w_down'][i])
        else:
            ff = _moe_swiglu(x, p['router_w'][i], p['router_b'][i], p['moe_w_gate'][i], p['moe_w_up'][i],
                             p['moe_w_down'][i])
        x = _layer_norm(ALPHA * x + ff, p['ln_ff_g'][layer], p['ln_ff_b'][layer])
    return x, jnp.stack(gla_new), jnp.stack(conf_new), jnp.stack(gdn_new), jnp.stack(gdnc_new)


def setup_inputs(seed: int = 0) -> dict:
    keys = iter(jax.random.split(jax.random.key(seed), 64))

    def normal(shape, scale):
        return jax.random.normal(next(keys), shape, F32) * scale

    def dense(shape, fan_in, gain=1.0):
        return normal(shape, gain * fan_in ** -0.5)

    def gain_vec(shape):
        return 1.0 + normal(shape, 0.02)

    def bias_vec(shape):
        return normal(shape, 0.02)

    d = D_MODEL
    inp = {}
    inp['x_prompt'] = normal((BATCH, SEQ, d), 1.0)
    inp['x_sample'] = normal((DEC_BATCH, DEC_SEQ, d), 1.0)
    inp['mem_prompt'] = normal((BATCH, MEM_LEN, d), 1.0)
    inp['state_gla'] = normal((N_EVEN, DEC_BATCH, GLA_HEADS, GLA_DK, GLA_DV), 0.5)
    inp['state_conf_conv'] = normal((N_EVEN, DEC_BATCH, CONV_W - 1, CONV_CH), 0.5)
    inp['state_gdn'] = normal((N_ODD, DEC_BATCH, GDN_V_HEADS, GDN_DK, GDN_DV), 0.1)
    inp['state_gdn_conv'] = normal((N_ODD, DEC_BATCH, GDN_CONV_W - 1, GDN_CONV_DIM), 0.5)
    inp['cache_mem_k'] = normal((DEPTH, DEC_BATCH, MEM_LEN, MEM_HEADS, MEM_HD), 1.0)
    inp['cache_mem_v'] = normal((DEPTH, DEC_BATCH, MEM_LEN, MEM_HEADS, MEM_HD), 1.0)
    inp['ln_mix_g'] = gain_vec((DEPTH, d))
    inp['ln_mix_b'] = bias_vec((DEPTH, d))
    inp['ln_mem_g'] = gain_vec((DEPTH, d))
    inp['ln_mem_b'] = bias_vec((DEPTH, d))
    inp['ln_ff_g'] = gain_vec((DEPTH, d))
    inp['ln_ff_b'] = bias_vec((DEPTH, d))
    inp['w_mq'] = dense((DEPTH, d, d), d)
    inp['w_mk'] = dense((DEPTH, d, d), d)
    inp['w_mv'] = dense((DEPTH, d, d), d, BETA_INIT)
    inp['w_mo'] = dense((DEPTH, d, d), d, BETA_INIT)
    inp['w_in_ab'] = dense((N_EVEN, d, AB_IN), d)
    inp['w_gk_up'] = dense((N_EVEN, GLA_GATE_RANK, GLA_QK), GLA_GATE_RANK)
    inp['b_gk'] = bias_vec((N_EVEN, GLA_QK))
    inp['gla_norm_g'] = gain_vec((N_EVEN, GLA_DV))
    inp['conv_dw_w'] = dense((N_EVEN, CONV_W, CONV_CH), CONV_W)
    inp['conv_dw_b'] = bias_vec((N_EVEN, CONV_CH))
    inp['conv_ln_g'] = gain_vec((N_EVEN, CONV_CH))
    inp['conv_ln_b'] = bias_vec((N_EVEN, CONV_CH))
    inp['w_out_ab'] = dense((N_EVEN, AB_MIX, d), AB_MIX, BETA_INIT)
    inp['ff_w_gate'] = dense((N_EVEN, d, FF_DENSE), d, BETA_INIT)
    inp['ff_w_up'] = dense((N_EVEN, d, FF_DENSE), d, BETA_INIT)
    inp['ff_w_down'] = dense((N_EVEN, FF_DENSE, d), FF_DENSE, BETA_INIT)
    inp['w_in_c'] = dense((N_ODD, d, C_IN), d)
    inp['gdn_conv_w'] = dense((N_ODD, GDN_CONV_W, GDN_CONV_DIM), GDN_CONV_W)
    inp['gdn_a_log'] = jnp.log(jax.random.uniform(next(keys), (N_ODD, GDN_V_HEADS), F32, 1.0, 16.0))
    dt = jnp.exp(jax.random.uniform(next(keys), (N_ODD, GDN_V_HEADS), F32, math.log(1e-3), math.log(1e-1)))
    inp['gdn_dt_bias'] = dt + jnp.log(-jnp.expm1(-dt))
    inp['gdn_norm_g'] = gain_vec((N_ODD, GDN_DV))
    inp['w_out_c'] = dense((N_ODD, GDN_V, d), GDN_V, BETA_INIT)
    inp['router_w'] = dense((N_ODD, d, N_EXPERTS), d)
    inp['router_b'] = normal((N_ODD, N_EXPERTS), 0.01)
    inp['moe_w_gate'] = dense((N_ODD, N_EXPERTS, d, FF_EXPERT), d, BETA_INIT)
    inp['moe_w_up'] = dense((N_ODD, N_EXPERTS, d, FF_EXPERT), d, BETA_INIT)
    inp['moe_w_down'] = dense((N_ODD, N_EXPERTS, FF_EXPERT, d), FF_EXPERT, BETA_INIT)
    return inp


def reference(x_prompt, x_sample, mem_prompt, state_gla, state_conf_conv, state_gdn, state_gdn_conv,
              cache_mem_k, cache_mem_v, ln_mix_g, ln_mix_b, ln_mem_g, ln_mem_b, ln_ff_g, ln_ff_b,
              w_mq, w_mk, w_mv, w_mo, w_in_ab, w_gk_up, b_gk, gla_norm_g, conv_dw_w, conv_dw_b,
              conv_ln_g, conv_ln_b, w_out_ab, ff_w_gate, ff_w_up, ff_w_down, w_in_c, gdn_conv_w,
              gdn_a_log, gdn_dt_bias, gdn_norm_g, w_out_c, router_w, router_b, moe_w_gate, moe_w_up,
              moe_w_down):
    p = dict(ln_mix_g=ln_mix_g, ln_mix_b=ln_mix_b, ln_mem_g=ln_mem_g, ln_mem_b=ln_mem_b,
             ln_ff_g=ln_ff_g, ln_ff_b=ln_ff_b, w_mq=w_mq, w_mo=w_mo, w_in_ab=w_in_ab, w_gk_up=w_gk_up,
             b_gk=b_gk, gla_norm_g=gla_norm_g, conv_dw_w=conv_dw_w, conv_dw_b=conv_dw_b,
             conv_ln_g=conv_ln_g, conv_ln_b=conv_ln_b, w_out_ab=w_out_ab, ff_w_gate=ff_w_gate,
             ff_w_up=ff_w_up, ff_w_down=ff_w_down, w_in_c=w_in_c, gdn_conv_w=gdn_conv_w,
             gdn_a_log=gdn_a_log, gdn_dt_bias=gdn_dt_bias, gdn_norm_g=gdn_norm_g, w_out_c=w_out_c,
             router_w=router_w, router_b=router_b, moe_w_gate=moe_w_gate, moe_w_up=moe_w_up,
             moe_w_down=moe_w_down)
    dt = x_prompt.dtype
    mem_k_p, mem_v_p = _memory_kv(mem_prompt, w_mk, w_mv)
    gla0 = jnp.zeros((N_EVEN, BATCH, GLA_HEADS, GLA_DK, GLA_DV), dt)
    conf0 = jnp.zeros((N_EVEN, BATCH, CONV_W - 1, CONV_CH), dt)
    gdn0 = jnp.zeros((N_ODD, BATCH, GDN_V_HEADS, GDN_DK, GDN_DV), dt)
    gdnc0 = jnp.zeros((N_ODD, BATCH, GDN_CONV_W - 1, GDN_CONV_DIM), dt)
    y_prompt, gla_p, conf_p, gdn_p, gdnc_p = _trunk(x_prompt, gla0, conf0, gdn0, gdnc0, mem_k_p, mem_v_p, p)
    y_sample, gla_s, conf_s, gdn_s, gdnc_s = _trunk(x_sample, state_gla, state_conf_conv, state_gdn,
                                                    state_gdn_conv, cache_mem_k, cache_mem_v, p)
    return (y_prompt, y_sample, gla_p, conf_p, gdn_p, gdnc_p, mem_k_p, mem_v_p, gla_s, conf_s, gdn_s, gdnc_s)
```

```python
import functools

import jax
import jax.numpy as jnp
from jax import lax
from jax.experimental import pallas as pl
from jax.experimental.pallas import tpu as pltpu

F32 = jnp.float32
BF16 = jnp.bfloat16

D_MODEL = 2048
DEPTH = 2
ALPHA = (2 * DEPTH) ** 0.25
LN_EPS = 1e-5
L2_EPS = 1e-6
CHUNK = 64

GLA_HEADS = 4
GLA_DK = 128
GLA_DV = 256
GLA_QK = GLA_HEADS * GLA_DK
GLA_V = GLA_HEADS * GLA_DV
GLA_GATE_RANK = 16
GLA_GATE_NORM = 16.0
GLA_SUB = 16
CONV_CH = 1024
CONV_W = 31
GDN_QK_HEADS = 16
GDN_V_HEADS = 32
GDN_DK = 128
GDN_DV = 128
GDN_CONV_W = 4
GDN_Q = GDN_QK_HEADS * GDN_DK
GDN_V = GDN_V_HEADS * GDN_DV
GDN_CONV_DIM = 2 * GDN_Q + GDN_V
MEM_LEN = 256
MEM_HEADS = 4
MEM_HD = D_MODEL // MEM_HEADS
N_EXPERTS = 8
LANES = 128
VMEM_LIMIT_BYTES = 56 * 1024 * 1024


def _cparams(*sem):
    return pltpu.CompilerParams(dimension_semantics=sem, vmem_limit_bytes=VMEM_LIMIT_BYTES)


def _dot(a, b):
    return jnp.dot(a.astype(BF16), b.astype(BF16), preferred_element_type=F32)


def _dot_nt(a, b):
    return lax.dot_general(a.astype(BF16), b.astype(BF16), (((1,), (1,)), ((), ())),
                           preferred_element_type=F32)


def _dot_tn(a, b):
    return lax.dot_general(a.astype(BF16), b.astype(BF16), (((0,), (0,)), ((), ())),
                           preferred_element_type=F32)


def _sigmoid(x):
    return 1.0 / (1.0 + jnp.exp(-x))


def _silu(x):
    return x * _sigmoid(x)


def _softplus(x):
    return jnp.maximum(x, 0.0) + jnp.log(1.0 + jnp.exp(-jnp.abs(x)))


def _layer_norm(y, g, b):
    mu = jnp.mean(y, axis=-1, keepdims=True)
    yc = y - mu
    var = jnp.mean(yc * yc, axis=-1, keepdims=True)
    return yc * lax.rsqrt(var + LN_EPS) * g + b


def _pick_tile(n, candidates):
    for c in candidates:
        if n % c == 0:
            return c
    raise ValueError(f"no tile for {n} in {candidates}")


def _mm_kernel(x_ref, w_ref, o_ref):
    o_ref[...] = jnp.dot(x_ref[...], w_ref[...], preferred_element_type=F32).astype(o_ref.dtype)


def _matmul(x, w, out_dtype):
    m, k = x.shape
    nl, _, n = w.shape
    tm = _pick_tile(m, (1024, 512))
    tn = _pick_tile(n, (1024, 512))
    return pl.pallas_call(
        _mm_kernel,
        grid=(nl, m // tm, n // tn),
        in_specs=[pl.BlockSpec((tm, k), lambda l, i, j: (i, 0)),
                  pl.BlockSpec((None, k, tn), lambda l, i, j: (l, 0, j))],
        out_specs=pl.BlockSpec((None, tm, tn), lambda l, i, j: (l, i, j)),
        out_shape=jax.ShapeDtypeStruct((nl, m, n), out_dtype),
        compiler_params=_cparams("parallel", "parallel", "arbitrary"),
    )(x, w)


def _mm_gated_kernel(*refs, mode, has_gate):
    if has_gate:
        x_ref, w1_ref, w2_ref, gate_ref, o_ref = refs
    else:
        x_ref, w1_ref, w2_ref, o_ref = refs
    x = x_ref[...]
    a = jnp.dot(x, w1_ref[...], preferred_element_type=F32)
    b = jnp.dot(x, w2_ref[...], preferred_element_type=F32)
    if mode == "glu":
        r = a * _sigmoid(b)
    else:
        r = _silu(a) * b
    if has_gate:
        gate = gate_ref[...]
        lane = lax.broadcasted_iota(jnp.int32, gate.shape, 1)
        r = r * jnp.sum(jnp.where(lane == pl.program_id(0), gate, 0.0), axis=1, keepdims=True)
    o_ref[...] = r.astype(o_ref.dtype)


def _mm_gated(x, w1, w2, n, off2, mode, out_dtype, gate=None):
    m, k = x.shape
    nl = w1.shape[0]
    tm = _pick_tile(m, (1024, 512))
    tn = 512
    nb = n // tn
    ob = off2 // tn
    in_specs = [pl.BlockSpec((tm, k), lambda l, i, j: (i, 0)),
                pl.BlockSpec((None, k, tn), lambda l, i, j: (l, 0, j)),
                pl.BlockSpec((None, k, tn), lambda l, i, j: (l, 0, ob + j))]
    args = [x, w1, w2]
    if gate is not None:
        in_specs.append(pl.BlockSpec((tm, LANES), lambda l, i, j: (i, 0)))
        args.append(gate)
    return pl.pallas_call(
        functools.partial(_mm_gated_kernel, mode=mode, has_gate=gate is not None),
        grid=(nl, m // tm, nb),
        in_specs=in_specs,
        out_specs=pl.BlockSpec((tm, tn), lambda l, i, j: (i, l * nb + j)),
        out_shape=jax.ShapeDtypeStruct((m, nl * n), out_dtype),
        compiler_params=_cparams("parallel", "parallel", "arbitrary"),
    )(*args)


def _mm_res_ln_kernel(x_ref, w_ref, res_ref, g_ref, b_ref, o_ref, obf_ref, acc_ref, *, nk):
    kk = pl.program_id(1)

    @pl.when(kk == 0)
    def _():
        acc_ref[...] = jnp.zeros_like(acc_ref)

    acc_ref[...] += jnp.dot(x_ref[...], w_ref[...], preferred_element_type=F32)

    @pl.when(kk == nk - 1)
    def _():
        y = _layer_norm(ALPHA * res_ref[...] + acc_ref[...], g_ref[...], b_ref[...])
        o_ref[...] = y
        obf_ref[...] = y.astype(BF16)


def _mm_res_ln(x, w, res, g, b):
    m, k = x.shape
    d = w.shape[1]
    tm = 512
    tk = _pick_tile(k, (1024, 512))
    nk = k // tk
    return pl.pallas_call(
        functools.partial(_mm_res_ln_kernel, nk=nk),
        grid=(m // tm, nk),
        in_specs=[pl.BlockSpec((tm, tk), lambda i, kk: (i, kk)),
                  pl.BlockSpec((tk, d), lambda i, kk: (kk, 0)),
                  pl.BlockSpec((tm, d), lambda i, kk: (i, 0)),
                  pl.BlockSpec((1, d), lambda i, kk: (0, 0)),
                  pl.BlockSpec((1, d), lambda i, kk: (0, 0))],
        out_specs=[pl.BlockSpec((tm, d), lambda i, kk: (i, 0)),
                   pl.BlockSpec((tm, d), lambda i, kk: (i, 0))],
        out_shape=[jax.ShapeDtypeStruct((m, d), F32), jax.ShapeDtypeStruct((m, d), BF16)],
        scratch_shapes=[pltpu.VMEM((tm, d), F32)],
        compiler_params=_cparams("parallel", "arbitrary"),
    )(x, w, res, g.reshape(1, d), b.reshape(1, d))


def _gla_gate_kernel(x_ref, wgl_ref, wup_ref, b_ref, o_ref):
    gl = jnp.dot(x_ref[...], wgl_ref[...], preferred_element_type=F32)
    z = _dot(gl, wup_ref[...]) + b_ref[...]
    o_ref[...] = (jnp.minimum(z, 0.0) - jnp.log(1.0 + jnp.exp(-jnp.abs(z)))) * (1.0 / GLA_GATE_NORM)


def _gla_gate(x, w_gl, w_up, b_gk):
    m, k = x.shape
    tm = 512
    wgl = jnp.zeros((k, LANES), BF16).at[:, :GLA_GATE_RANK].set(w_gl.astype(BF16))
    wup = jnp.zeros((LANES, GLA_QK), BF16).at[:GLA_GATE_RANK].set(w_up.astype(BF16))
    return pl.pallas_call(
        _gla_gate_kernel,
        grid=(m // tm,),
        in_specs=[pl.BlockSpec((tm, k), lambda i: (i, 0)),
                  pl.BlockSpec((k, LANES), lambda i: (0, 0)),
                  pl.BlockSpec((LANES, GLA_QK), lambda i: (0, 0)),
                  pl.BlockSpec((1, GLA_QK), lambda i: (0, 0))],
        out_specs=pl.BlockSpec((tm, GLA_QK), lambda i: (i, 0)),
        out_shape=jax.ShapeDtypeStruct((m, GLA_QK), F32),
        compiler_params=_cparams("parallel"),
    )(x, wgl, wup, b_gk.reshape(1, GLA_QK).astype(F32))


def _gdn_gate_kernel(x_ref, w_ref, alog_ref, dtb_ref, o_ref):
    p = jnp.dot(x_ref[...], w_ref[...], preferred_element_type=F32)
    lane = lax.broadcasted_iota(jnp.int32, p.shape, 1)
    beta = _sigmoid(p)
    g = -jnp.exp(alog_ref[...]) * _softplus(p + dtb_ref[...])
    o_ref[...] = jnp.where(lane < GDN_V_HEADS, beta, jnp.where(lane < 2 * GDN_V_HEADS, g, 0.0))


def _gdn_gate(x, w_ba, a_log, dt_bias):
    m, k = x.shape
    tm = 512
    hv = GDN_V_HEADS
    w = jnp.zeros((k, LANES), BF16).at[:, :2 * hv].set(w_ba.astype(BF16))
    alog = jnp.zeros((1, LANES), F32).at[0, hv:2 * hv].set(a_log.astype(F32))
    dtb = jnp.zeros((1, LANES), F32).at[0, hv:2 * hv].set(dt_bias.astype(F32))
    return pl.pallas_call(
        _gdn_gate_kernel,
        grid=(m // tm,),
        in_specs=[pl.BlockSpec((tm, k), lambda i: (i, 0)),
                  pl.BlockSpec((k, LANES), lambda i: (0, 0)),
                  pl.BlockSpec((1, LANES), lambda i: (0, 0)),
                  pl.BlockSpec((1, LANES), lambda i: (0, 0))],
        out_specs=pl.BlockSpec((tm, LANES), lambda i: (i, 0)),
        out_shape=jax.ShapeDtypeStruct((m, LANES), F32),
        compiler_params=_cparams("parallel"),
    )(x, w, alog, dtb)


def _router_kernel(x_ref, w_ref, b_ref, o_ref):
    logits = jnp.dot(x_ref[...], w_ref[...], preferred_element_type=F32) + b_ref[...]
    lane = lax.broadcasted_iota(jnp.int32, logits.shape, 1)
    neg = -jnp.inf
    logits = jnp.where(lane < N_EXPERTS, logits, neg)
    m1 = jnp.max(logits, axis=1, keepdims=True)
    i1 = jnp.min(jnp.where(logits == m1, lane, LANES), axis=1, keepdims=True)
    rest = jnp.where(lane == i1, neg, logits)
    m2 = jnp.max(rest, axis=1, keepdims=True)
    i2 = jnp.min(jnp.where(rest == m2, lane, LANES), axis=1, keepdims=True)
    e = jnp.exp(m2 - m1)
    denom = 1.0 + e
    o_ref[...] = jnp.where(lane == i1, 1.0 / denom, 0.0) + jnp.where(lane == i2, e / denom, 0.0)


def _router(x, router_w, router_b):
    m, k = x.shape
    tm = 512
    w = jnp.zeros((k, LANES), BF16).at[:, :N_EXPERTS].set(router_w.astype(BF16))
    b = jnp.zeros((1, LANES), F32).at[0, :N_EXPERTS].set(router_b.astype(F32))
    return pl.pallas_call(
        _router_kernel,
        grid=(m // tm,),
        in_specs=[pl.BlockSpec((tm, k), lambda i: (i, 0)),
                  pl.BlockSpec((k, LANES), lambda i: (0, 0)),
                  pl.BlockSpec((1, LANES), lambda i: (0, 0))],
        out_specs=pl.BlockSpec((tm, LANES), lambda i: (i, 0)),
        out_shape=jax.ShapeDtypeStruct((m, LANES), F32),
        compiler_params=_cparams("parallel"),
    )(x, w, b)


def _gla_kernel(q_ref, k_ref, v_ref, la_ref, r_ref, s0_ref, ng_ref, o_ref, sout_ref, st_ref, bc_ref,
                *, tt, nt):
    sub = GLA_SUB
    i = pl.program_id(2)

    @pl.when(i == 0)
    def _():
        st_ref[...] = s0_ref[...].T

    rows = lax.broadcasted_iota(jnp.int32, (tt, tt), 0)
    cols = lax.broadcasted_iota(jnp.int32, (tt, tt), 1)
    tri = jnp.where(((rows & -sub) == (cols & -sub)) & (cols <= rows), 1.0, 0.0).astype(F32)
    bc_ref[...] = jnp.dot(tri, la_ref[...], precision=lax.Precision.HIGHEST, preferred_element_type=F32)
    trow = lax.broadcasted_iota(jnp.int32, (sub, GLA_DK), 0)
    lane = lax.broadcasted_iota(jnp.int32, (sub, LANES), 1)
    ng = ng_ref[...]

    def chunk(c, carry):
        sl = pl.ds(pl.multiple_of(c * sub, sub), sub)
        b = bc_ref[sl, :]
        q = q_ref[sl, :] * (GLA_DK ** -0.5)
        k = k_ref[sl, :]
        v = v_ref[sl, :]
        st = st_ref[...]
        o = _dot_nt(q * jnp.exp(b), st)
        att = jnp.zeros((sub, LANES), F32)
        for s in range(sub):
            e = jnp.exp(jnp.where(trow >= s, b - b[s:s + 1, :], -jnp.inf))
            col = jnp.sum(q * k[s:s + 1, :] * e, axis=1, keepdims=True)
            att = jnp.where(lane == s, col, att)
        o = o + _dot(att[:, :sub], v)
        b_last = b[sub - 1:sub, :]
        st_ref[...] = st * jnp.exp(b_last) + _dot_tn(v, k * jnp.exp(b_last - b))
        ms = jnp.mean(o * o, axis=1, keepdims=True)
        o_ref[sl, :] = (o * lax.rsqrt(ms + LN_EPS) * ng * _silu(r_ref[sl, :])).astype(o_ref.dtype)
        return carry

    lax.fori_loop(0, tt // sub, chunk, 0)

    @pl.when(i == nt - 1)
    def _():
        sout_ref[...] = st_ref[...].T


def _gla(qkvr, log_a, s0, norm_g, bsz, t):
    tt = _pick_tile(t, (128, 16))
    nt = t // tt
    h = GLA_HEADS
    vb = GLA_QK * 2 // GLA_DV
    rb = vb + GLA_V // GLA_DV
    row = lambda b, hh, i: b * nt + i
    return pl.pallas_call(
        functools.partial(_gla_kernel, tt=tt, nt=nt),
        grid=(bsz, h, nt),
        in_specs=[pl.BlockSpec((tt, GLA_DK), lambda b, hh, i: (row(b, hh, i), hh)),
                  pl.BlockSpec((tt, GLA_DK), lambda b, hh, i: (row(b, hh, i), h + hh)),
                  pl.BlockSpec((tt, GLA_DV), lambda b, hh, i: (row(b, hh, i), vb + hh)),
                  pl.BlockSpec((tt, GLA_DK), lambda b, hh, i: (row(b, hh, i), hh)),
                  pl.BlockSpec((tt, GLA_DV), lambda b, hh, i: (row(b, hh, i), rb + hh)),
                  pl.BlockSpec((None, None, GLA_DK, GLA_DV), lambda b, hh, i: (b, hh, 0, 0)),
                  pl.BlockSpec((1, GLA_DV), lambda b, hh, i: (0, 0))],
        out_specs=[pl.BlockSpec((tt, GLA_DV), lambda b, hh, i: (row(b, hh, i), hh)),
                   pl.BlockSpec((None, None, GLA_DK, GLA_DV), lambda b, hh, i: (b, hh, 0, 0))],
        out_shape=[jax.ShapeDtypeStruct((bsz * t, GLA_V), BF16),
                   jax.ShapeDtypeStruct((bsz, h, GLA_DK, GLA_DV), F32)],
        scratch_shapes=[pltpu.VMEM((GLA_DV, GLA_DK), F32), pltpu.VMEM((tt, GLA_DK), F32)],
        compiler_params=_cparams("parallel", "parallel", "arbitrary"),
    )(qkvr, qkvr, qkvr, log_a, qkvr, s0, norm_g.reshape(1, GLA_DV).astype(F32))


def _conf_conv_kernel(u_ref, buf_ref, w_ref, dwb_ref, g_ref, b_ref, c_ref, bufo_ref, ext_ref, *, tt, nt):
    hist = CONV_W - 1
    pad = 32 - hist
    i = pl.program_id(1)

    @pl.when(i == 0)
    def _():
        ext_ref[0:pad, :] = jnp.zeros((pad, CONV_CH), F32)
        ext_ref[pad:32, :] = buf_ref[...]

    ext_ref[32:32 + tt, :] = u_ref[...]
    cw = 256
    for cb in range(CONV_CH // cw):
        cs = slice(cb * cw, (cb + 1) * cw)
        acc = jnp.zeros((tt, cw), F32)
        for j in range(CONV_W):
            acc = acc + w_ref[j:j + 1, cs] * ext_ref[pad + j:pad + j + tt, cs]
        c_ref[:, cs] = (acc + dwb_ref[:, cs]).astype(c_ref.dtype)
    y = _layer_norm(c_ref[...].astype(F32), g_ref[...], b_ref[...])
    c_ref[...] = _silu(y).astype(c_ref.dtype)

    @pl.when(i == nt - 1)
    def _():
        bufo_ref[...] = ext_ref[tt + pad:tt + 32, :]

    ext_ref[0:32, :] = ext_ref[tt:tt + 32, :]


def _conf_conv(u, buf, dw_w, dw_b, ln_g, ln_b, bsz, t):
    tt = _pick_tile(t, (128, 16))
    nt = t // tt
    vec = lambda a: a.reshape(1, CONV_CH).astype(F32)
    return pl.pallas_call(
        functools.partial(_conf_conv_kernel, tt=tt, nt=nt),
        grid=(bsz, nt),
        in_specs=[pl.BlockSpec((tt, CONV_CH), lambda b, i: (b * nt + i, 0)),
                  pl.BlockSpec((None, CONV_W - 1, CONV_CH), lambda b, i: (b, 0, 0)),
                  pl.BlockSpec((CONV_W, CONV_CH), lambda b, i: (0, 0)),
                  pl.BlockSpec((1, CONV_CH), lambda b, i: (0, 0)),
                  pl.BlockSpec((1, CONV_CH), lambda b, i: (0, 0)),
                  pl.BlockSpec((1, CONV_CH), lambda b, i: (0, 0))],
        out_specs=[pl.BlockSpec((tt, CONV_CH), lambda b, i: (b * nt + i, 0)),
                   pl.BlockSpec((None, CONV_W - 1, CONV_CH), lambda b, i: (b, 0, 0))],
        out_shape=[jax.ShapeDtypeStruct((bsz * t, CONV_CH), F32),
                   jax.ShapeDtypeStruct((bsz, CONV_W - 1, CONV_CH), F32)],
        scratch_shapes=[pltpu.VMEM((32 + tt, CONV_CH), F32)],
        compiler_params=_cparams("parallel", "arbitrary"),
    )(u, buf, dw_w.astype(F32), vec(dw_b), vec(ln_g), vec(ln_b))


def _gdn_conv_kernel(x_ref, buf_ref, w_ref, o_ref, bufo_ref, ext_ref, *, tt, nt, tc):
    hist = GDN_CONV_W - 1
    pad = 8 - hist
    cb = pl.program_id(1)
    i = pl.program_id(2)

    @pl.when(i == 0)
    def _():
        ext_ref[0:pad, :] = jnp.zeros((pad, tc), F32)
        ext_ref[pad:8, :] = buf_ref[...]

    ext_ref[8:8 + tt, :] = x_ref[...]
    acc = jnp.zeros((tt, tc), F32)
    for j in range(GDN_CONV_W):
        acc = acc + w_ref[j:j + 1, :] * ext_ref[pad + j:pad + j + tt, :]
    y = _silu(acc)
    is_qk = cb < (2 * GDN_Q) // tc
    scale = jnp.where(cb < GDN_Q // tc, GDN_DK ** -0.5, 1.0)
    for hh in range(tc // GDN_DK):
        cs = slice(hh * GDN_DK, (hh + 1) * GDN_DK)
        seg = y[:, cs]
        nrm = seg * lax.rsqrt(jnp.sum(seg * seg, axis=1, keepdims=True) + L2_EPS) * scale
        o_ref[:, cs] = jnp.where(is_qk, nrm, seg)

    @pl.when(i == nt - 1)
    def _():
        bufo_ref[...] = ext_ref[tt + pad:tt + 8, :]

    ext_ref[0:8, :] = ext_ref[tt:tt + 8, :]


def _gdn_conv(proj, buf, conv_w, bsz, t):
    tt = _pick_tile(t, (256, 16))
    nt = t // tt
    tc = 1024
    return pl.pallas_call(
        functools.partial(_gdn_conv_kernel, tt=tt, nt=nt, tc=tc),
        grid=(bsz, GDN_CONV_DIM // tc, nt),
        in_specs=[pl.BlockSpec((tt, tc), lambda b, c, i: (b * nt + i, c)),
                  pl.BlockSpec((None, GDN_CONV_W - 1, tc), lambda b, c, i: (b, 0, c)),
                  pl.BlockSpec((GDN_CONV_W, tc), lambda b, c, i: (0, c))],
        out_specs=[pl.BlockSpec((tt, tc), lambda b, c, i: (b * nt + i, c)),
                   pl.BlockSpec((None, GDN_CONV_W - 1, tc), lambda b, c, i: (b, 0, c))],
        out_shape=[jax.ShapeDtypeStruct((bsz * t, GDN_CONV_DIM), F32),
                   jax.ShapeDtypeStruct((bsz, GDN_CONV_W - 1, GDN_CONV_DIM), F32)],
        scratch_shapes=[pltpu.VMEM((8 + tt, tc), F32)],
        compiler_params=_cparams("parallel", "parallel", "arbitrary"),
    )(proj, buf, conv_w.astype(F32))


def _unit_lower_inverse(m, size):
    rows = lax.broadcasted_iota(jnp.int32, (size, size), 0)
    cols = lax.broadcasted_iota(jnp.int32, (size, size), 1)
    acc = jnp.where(rows == cols, 1.0, 0.0) - m
    power = _dot(m, m)
    terms = 2
    while terms < size:
        new_acc = acc + _dot(power, acc)
        if 2 * terms < size:
            power = _dot(power, power)
        acc = new_acc
        terms *= 2
    return acc


def _gdn_kernel(q_ref, k_ref, v_ref, z_ref, gb_ref, s0_ref, ng_ref, o_ref, sout_ref, s_ref,
                *, chunk, tt, nt):
    hq = pl.program_id(1)
    i = pl.program_id(2)
    rep = GDN_V_HEADS // GDN_QK_HEADS

    @pl.when(i == 0)
    def _():
        s_ref[...] = s0_ref[...]

    lane = lax.broadcasted_iota(jnp.int32, (chunk, LANES), 1)
    rows = lax.broadcasted_iota(jnp.int32, (chunk, chunk), 0)
    cols = lax.broadcasted_iota(jnp.int32, (chunk, chunk), 1)
    incl = cols <= rows
    strict = cols < rows
    ng = ng_ref[...]

    def body(c, carry):
        sl = pl.ds(pl.multiple_of(c * chunk, chunk), chunk)
        q = q_ref[sl, :]
        k = k_ref[sl, :]
        gb = gb_ref[sl, :]
        kk = _dot_nt(k, k)
        qk = _dot_nt(q, k)
        for j in range(rep):
            hv = rep * hq + j
            vs = slice(j * GDN_DV, (j + 1) * GDN_DV)
            beta = jnp.sum(jnp.where(lane == hv, gb, 0.0), axis=1, keepdims=True)
            g = jnp.sum(jnp.where(lane == GDN_V_HEADS + hv, gb, 0.0), axis=1, keepdims=True)
            g_row = jnp.sum(jnp.where(rows == cols, g, 0.0), axis=0, keepdims=True)
            gcum = jnp.sum(jnp.where(incl, g_row, 0.0), axis=1, keepdims=True)
            gcum_row = jnp.sum(jnp.where(rows <= cols, g, 0.0), axis=0, keepdims=True)
            rel = jnp.exp(jnp.where(incl, gcum - gcum_row, -jnp.inf))
            tinv = _unit_lower_inverse(jnp.where(strict, kk * beta * rel, 0.0), chunk)
            eg = jnp.exp(gcum)
            v = v_ref[sl, vs]
            u = _dot(tinv, v * beta)
            w = _dot(tinv, k * (beta * eg))
            s = s_ref[j]
            v_new = u - _dot(w, s)
            o = _dot(q * eg, s) + _dot(qk * rel, v_new)
            g_last = gcum[chunk - 1:chunk, :]
            s_ref[j] = s * jnp.exp(g_last) + _dot_tn(k * jnp.exp(g_last - gcum), v_new)
            ms = jnp.mean(o * o, axis=1, keepdims=True)
            o_ref[sl, vs] = (o * lax.rsqrt(ms + LN_EPS) * ng * _silu(z_ref[sl, vs])).astype(o_ref.dtype)
        return carry

    lax.fori_loop(0, tt // chunk, body, 0)

    @pl.when(i == nt - 1)
    def _():
        sout_ref[...] = s_ref[...]


def _gdn(qkv, proj, gb, s0, norm_g, bsz, t):
    chunk = min(CHUNK, t)
    tt = _pick_tile(t, (256, chunk))
    nt = t // tt
    rep = GDN_V_HEADS // GDN_QK_HEADS
    vw = rep * GDN_DV
    hq_n = GDN_QK_HEADS
    row = lambda b, i: b * nt + i
    return pl.pallas_call(
        functools.partial(_gdn_kernel, chunk=chunk, tt=tt, nt=nt),
        grid=(bsz, hq_n, nt),
        in_specs=[pl.BlockSpec((tt, GDN_DK), lambda b, h, i: (row(b, i), h)),
                  pl.BlockSpec((tt, GDN_DK), lambda b, h, i: (row(b, i), hq_n + h)),
                  pl.BlockSpec((tt, vw), lambda b, h, i: (row(b, i), 2 * GDN_Q // vw + h)),
                  pl.BlockSpec((tt, vw), lambda b, h, i: (row(b, i), GDN_CONV_DIM // vw + h)),
                  pl.BlockSpec((tt, LANES), lambda b, h, i: (row(b, i), 0)),
                  pl.BlockSpec((None, rep, GDN_DK, GDN_DV), lambda b, h, i: (b, h, 0, 0)),
                  pl.BlockSpec((1, GDN_DV), lambda b, h, i: (0, 0))],
        out_specs=[pl.BlockSpec((tt, vw), lambda b, h, i: (row(b, i), h)),
                   pl.BlockSpec((None, rep, GDN_DK, GDN_DV), lambda b, h, i: (b, h, 0, 0))],
        out_shape=[jax.ShapeDtypeStruct((bsz * t, GDN_V), BF16),
                   jax.ShapeDtypeStruct((bsz, GDN_V_HEADS, GDN_DK, GDN_DV), F32)],
        scratch_shapes=[pltpu.VMEM((rep, GDN_DK, GDN_DV), F32)],
        compiler_params=_cparams("parallel", "parallel", "arbitrary"),
    )(qkv, qkv, qkv, proj, gb, s0, norm_g.reshape(1, GDN_DV).astype(F32))


def _mem_attn_kernel(q_ref, k_ref, v_ref, o_ref):
    for h in range(MEM_HEADS):
        hs = slice(h * MEM_HD, (h + 1) * MEM_HD)
        s = _dot_nt(q_ref[:, hs], k_ref[:, hs]) * (MEM_HD ** -0.5)
        s = s - jnp.max(s, axis=1, keepdims=True)
        p = jnp.exp(s)
        p = p / jnp.sum(p, axis=1, keepdims=True)
        o_ref[:, hs] = _dot(p, v_ref[:, hs]).astype(o_ref.dtype)


def _mem_attn(q, mk, mv, bsz, t):
    tq = _pick_tile(t, (512, 16))
    nt = t // tq
    return pl.pallas_call(
        _mem_attn_kernel,
        grid=(bsz, nt),
        in_specs=[pl.BlockSpec((tq, D_MODEL), lambda b, i: (b * nt + i, 0)),
                  pl.BlockSpec((None, MEM_LEN, D_MODEL), lambda b, i: (b, 0, 0)),
                  pl.BlockSpec((None, MEM_LEN, D_MODEL), lambda b, i: (b, 0, 0))],
        out_specs=pl.BlockSpec((tq, D_MODEL), lambda b, i: (b * nt + i, 0)),
        out_shape=jax.ShapeDtypeStruct((bsz * t, D_MODEL), BF16),
        compiler_params=_cparams("parallel", "arbitrary"),
    )(q, mk, mv)


def _trunk(x, bsz, t, gla_s, conf_buf, gdn_s, gdn_buf, mem_k, mem_v, p):
    m = bsz * t
    xb = x.astype(BF16)

    qkvr = _matmul(xb, p["w_qkvr"], F32)[0]
    log_a = _gla_gate(xb, p["w_gl"], p["w_gk_up"], p["b_gk"])
    u = _mm_gated(xb, p["w_glu"], p["w_glu"], CONV_CH, CONV_CH, "glu", F32)
    o_a, gla_new = _gla(qkvr, log_a, gla_s, p["gla_norm_g"], bsz, t)
    c, conf_new = _conf_conv(u, conf_buf, p["conv_dw_w"], p["conv_dw_b"], p["conv_ln_g"], p["conv_ln_b"], bsz, t)
    mix = jnp.concatenate([o_a, c.astype(BF16)], axis=1)
    x, xb = _mm_res_ln(mix, p["w_out_ab"], x, p["ln_mix_g"][0], p["ln_mix_b"][0])
    qm = _matmul(xb, p["w_mq"][0:1], BF16)[0]
    att = _mem_attn(qm, mem_k[0], mem_v[0], bsz, t)
    x, xb = _mm_res_ln(att, p["w_mo"][0], x, p["ln_mem_g"][0], p["ln_mem_b"][0])
    hdn = _mm_gated(xb, p["ff_w_gate"], p["ff_w_up"], p["ff_w_gate"].shape[2], 0, "swiglu", BF16)
    x, xb = _mm_res_ln(hdn, p["ff_w_down"], x, p["ln_ff_g"][0], p["ln_ff_b"][0])

    proj = _matmul(xb, p["w_qkvz"], F32)[0]
    gb = _gdn_gate(xb, p["w_ba"], p["gdn_a_log"], p["gdn_dt_bias"])
    qkv, gdnc_new = _gdn_conv(proj, gdn_buf, p["gdn_conv_w"], bsz, t)
    o_c, gdn_new = _gdn(qkv, proj, gb, gdn_s, p["gdn_norm_g"], bsz, t)
    x, xb = _mm_res_ln(o_c, p["w_out_c"], x, p["ln_mix_g"][1], p["ln_mix_b"][1])
    qm = _matmul(xb, p["w_mq"][1:2], BF16)[0]
    att = _mem_attn(qm, mem_k[1], mem_v[1], bsz, t)
    x, xb = _mm_res_ln(att, p["w_mo"][1], x, p["ln_mem_g"][1], p["ln_mem_b"][1])
    gate = _router(xb, p["router_w"], p["router_b"])
    ff_e = p["moe_w_gate"].shape[2]
    hdn = _mm_gated(xb, p["moe_w_gate"], p["moe_w_up"], ff_e, 0, "swiglu", BF16, gate=gate)
    x, xb = _mm_res_ln(hdn, p["moe_w_down"], x, p["ln_ff_g"][1], p["ln_ff_b"][1])
    return x, gla_new, conf_new, gdn_new, gdnc_new


def kernel(x_prompt, x_sample, mem_prompt, state_gla, state_conf_conv, state_gdn, state_gdn_conv, cache_mem_k, cache_mem_v, ln_mix_g, ln_mix_b, ln_mem_g, ln_mem_b, ln_ff_g, ln_ff_b, w_mq, w_mk, w_mv, w_mo, w_in_ab, w_gk_up, b_gk, gla_norm_g, conv_dw_w, conv_dw_b, conv_ln_g, conv_ln_b, w_out_ab, ff_w_gate, ff_w_up, ff_w_down, w_in_c, gdn_conv_w, gdn_a_log, gdn_dt_bias, gdn_norm_g, w_out_c, router_w, router_b, moe_w_gate, moe_w_up, moe_w_down):
    bf = lambda a: a.astype(BF16)
    qkvr_n = 2 * GLA_QK + 2 * GLA_V
    glu_0 = qkvr_n + GLA_GATE_RANK
    qkvz_n = GDN_CONV_DIM + GDN_V
    p = dict(
        ln_mix_g=ln_mix_g, ln_mix_b=ln_mix_b, ln_mem_g=ln_mem_g, ln_mem_b=ln_mem_b, ln_ff_g=ln_ff_g, ln_ff_b=ln_ff_b,
        w_mq=bf(w_mq), w_mo=bf(w_mo),
        w_qkvr=bf(w_in_ab[:, :, :qkvr_n]), w_gl=w_in_ab[0, :, qkvr_n:glu_0], w_glu=bf(w_in_ab[:, :, glu_0:]),
        w_gk_up=w_gk_up[0], b_gk=b_gk[0], gla_norm_g=gla_norm_g[0],
        conv_dw_w=conv_dw_w[0], conv_dw_b=conv_dw_b[0], conv_ln_g=conv_ln_g[0], conv_ln_b=conv_ln_b[0],
        w_out_ab=bf(w_out_ab[0]), ff_w_gate=bf(ff_w_gate), ff_w_up=bf(ff_w_up), ff_w_down=bf(ff_w_down[0]),
        w_qkvz=bf(w_in_c[:, :, :qkvz_n]), w_ba=w_in_c[0, :, qkvz_n:], gdn_conv_w=gdn_conv_w[0],
        gdn_a_log=gdn_a_log[0], gdn_dt_bias=gdn_dt_bias[0], gdn_norm_g=gdn_norm_g[0], w_out_c=bf(w_out_c[0]),
        router_w=router_w[0], router_b=router_b[0],
        moe_w_gate=bf(moe_w_gate[0]), moe_w_up=bf(moe_w_up[0]),
        moe_w_down=bf(moe_w_down[0]).reshape(-1, D_MODEL),
    )
    nb, nt, d = x_prompt.shape
    sb, st, _ = x_sample.shape
    dt = x_prompt.dtype

    mem = bf(mem_prompt.reshape(nb * MEM_LEN, d))
    mem_k_p = _matmul(mem, bf(w_mk), F32)
    mem_v_p = _matmul(mem, bf(w_mv), F32)
    y_p, gla_p, conf_p, gdn_p, gdnc_p = _trunk(
        x_prompt.reshape(nb * nt, d), nb, nt,
        jnp.zeros((nb, GLA_HEADS, GLA_DK, GLA_DV), dt), jnp.zeros((nb, CONV_W - 1, CONV_CH), dt),
        jnp.zeros((nb, GDN_V_HEADS, GDN_DK, GDN_DV), dt), jnp.zeros((nb, GDN_CONV_W - 1, GDN_CONV_DIM), dt),
        mem_k_p.reshape(DEPTH, nb, MEM_LEN, d), mem_v_p.reshape(DEPTH, nb, MEM_LEN, d), p)

    y_s, gla_s, conf_s, gdn_s, gdnc_s = _trunk(
        x_sample.reshape(sb * st, d), sb, st, state_gla[0], state_conf_conv[0], state_gdn[0], state_gdn_conv[0],
        cache_mem_k.reshape(DEPTH, sb, MEM_LEN, d), cache_mem_v.reshape(DEPTH, sb, MEM_LEN, d), p)

    kv_shape = (DEPTH, nb, MEM_LEN, MEM_HEADS, MEM_HD)
    return (y_p.reshape(nb, nt, d), y_s.reshape(sb, st, d),
            gla_p[None], conf_p[None], gdn_p[None], gdnc_p[None],
            mem_k_p.reshape(kv_shape), mem_v_p.reshape(kv_shape),
            gla_s[None], conf_s[None], gdn_s[None], gdnc_s[None])
```

```python
import functools

import jax
import jax.numpy as jnp
from jax import lax
from jax.experimental import pallas as pl
from jax.experimental.pallas import tpu as pltpu

F32 = jnp.float32
BF16 = jnp.bfloat16

D_MODEL = 2048
DEPTH = 2
ALPHA = (2 * DEPTH) ** 0.25
LN_EPS = 1e-5
L2_EPS = 1e-6
CHUNK = 64

GLA_HEADS = 4
GLA_DK = 128
GLA_DV = 256
GLA_QK = GLA_HEADS * GLA_DK
GLA_V = GLA_HEADS * GLA_DV
GLA_GATE_RANK = 16
GLA_GATE_NORM = 16.0
GLA_SUB = 16
CONV_CH = 1024
CONV_W = 31
GDN_QK_HEADS = 16
GDN_V_HEADS = 32
GDN_DK = 128
GDN_DV = 128
GDN_CONV_W = 4
GDN_Q = GDN_QK_HEADS * GDN_DK
GDN_V = GDN_V_HEADS * GDN_DV
GDN_CONV_DIM = 2 * GDN_Q + GDN_V
MEM_LEN = 256
MEM_HEADS = 4
MEM_HD = D_MODEL // MEM_HEADS
N_EXPERTS = 8
LANES = 128
VMEM_LIMIT_BYTES = 56 * 1024 * 1024


def _cparams(*sem):
    return pltpu.CompilerParams(dimension_semantics=sem, vmem_limit_bytes=VMEM_LIMIT_BYTES)


def _dot(a, b):
    return jnp.dot(a.astype(BF16), b.astype(BF16), preferred_element_type=F32)


def _dot_nt(a, b):
    return lax.dot_general(a.astype(BF16), b.astype(BF16), (((1,), (1,)), ((), ())),
                           preferred_element_type=F32)


def _dot_tn(a, b):
    return lax.dot_general(a.astype(BF16), b.astype(BF16), (((0,), (0,)), ((), ())),
                           preferred_element_type=F32)


def _sigmoid(x):
    return 1.0 / (1.0 + jnp.exp(-x))


def _silu(x):
    return x * _sigmoid(x)


def _softplus(x):
    return jnp.maximum(x, 0.0) + jnp.log(1.0 + jnp.exp(-jnp.abs(x)))


def _layer_norm(y, g, b):
    mu = jnp.mean(y, axis=-1, keepdims=True)
    yc = y - mu
    var = jnp.mean(yc * yc, axis=-1, keepdims=True)
    return yc * lax.rsqrt(var + LN_EPS) * g + b


def _pick_tile(n, candidates):
    for c in candidates:
        if n % c == 0:
            return c
    raise ValueError(f"no tile for {n} in {candidates}")


def _mm_kernel(x_ref, w_ref, o_ref):
    o_ref[...] = jnp.dot(x_ref[...], w_ref[...], preferred_element_type=F32).astype(o_ref.dtype)


def _matmul(x, w, out_dtype):
    m, k = x.shape
    nl, _, n = w.shape
    tm = _pick_tile(m, (1024, 512))
    tn = _pick_tile(n, (1024, 512))
    return pl.pallas_call(
        _mm_kernel,
        grid=(nl, m // tm, n // tn),
        in_specs=[pl.BlockSpec((tm, k), lambda l, i, j: (i, 0)),
                  pl.BlockSpec((None, k, tn), lambda l, i, j: (l, 0, j))],
        out_specs=pl.BlockSpec((None, tm, tn), lambda l, i, j: (l, i, j)),
        out_shape=jax.ShapeDtypeStruct((nl, m, n), out_dtype),
        compiler_params=_cparams("parallel", "parallel", "arbitrary"),
    )(x, w)


def _mm_gated_kernel(*refs, mode, has_gate):
    if has_gate:
        x_ref, w1_ref, w2_ref, gate_ref, o_ref = refs
    else:
        x_ref, w1_ref, w2_ref, o_ref = refs
    x = x_ref[...]
    a = jnp.dot(x, w1_ref[...], preferred_element_type=F32)
    b = jnp.dot(x, w2_ref[...], preferred_element_type=F32)
    if mode == "glu":
        r = a * _sigmoid(b)
    else:
        r = _silu(a) * b
    if has_gate:
        gate = gate_ref[...]
        lane = lax.broadcasted_iota(jnp.int32, gate.shape, 1)
        r = r * jnp.sum(jnp.where(lane == pl.program_id(0), gate, 0.0), axis=1, keepdims=True)
    o_ref[...] = r.astype(o_ref.dtype)


def _mm_gated(x, w1, w2, n, off2, mode, out_dtype, gate=None):
    m, k = x.shape
    nl = w1.shape[0]
    tm = _pick_tile(m, (1024, 512))
    tn = 512
    nb = n // tn
    ob = off2 // tn
    in_specs = [pl.BlockSpec((tm, k), lambda l, i, j: (i, 0)),
                pl.BlockSpec((None, k, tn), lambda l, i, j: (l, 0, j)),
                pl.BlockSpec((None, k, tn), lambda l, i, j: (l, 0, ob + j))]
    args = [x, w1, w2]
    if gate is not None:
        in_specs.append(pl.BlockSpec((tm, LANES), lambda l, i, j: (i, 0)))
        args.append(gate)
    return pl.pallas_call(
        functools.partial(_mm_gated_kernel, mode=mode, has_gate=gate is not None),
        grid=(nl, m // tm, nb),
        in_specs=in_specs,
        out_specs=pl.BlockSpec((tm, tn), lambda l, i, j: (i, l * nb + j)),
        out_shape=jax.ShapeDtypeStruct((m, nl * n), out_dtype),
        compiler_params=_cparams("parallel", "parallel", "arbitrary"),
    )(*args)


def _mm_res_ln_kernel(x_ref, w_ref, res_ref, g_ref, b_ref, o_ref, obf_ref, acc_ref, *, nk):
    kk = pl.program_id(1)

    @pl.when(kk == 0)
    def _():
        acc_ref[...] = jnp.zeros_like(acc_ref)

    acc_ref[...] += jnp.dot(x_ref[...], w_ref[...], preferred_element_type=F32)

    @pl.when(kk == nk - 1)
    def _():
        y = _layer_norm(ALPHA * res_ref[...] + acc_ref[...], g_ref[...], b_ref[...])
        o_ref[...] = y
        obf_ref[...] = y.astype(BF16)


def _mm_res_ln(x, w, res, g, b):
    m, k = x.shape
    d = w.shape[1]
    tm = 512
    tk = _pick_tile(k, (1024, 512))
    nk = k // tk
    return pl.pallas_call(
        functools.partial(_mm_res_ln_kernel, nk=nk),
        grid=(m // tm, nk),
        in_specs=[pl.BlockSpec((tm, tk), lambda i, kk: (i, kk)),
                  pl.BlockSpec((tk, d), lambda i, kk: (kk, 0)),
                  pl.BlockSpec((tm, d), lambda i, kk: (i, 0)),
                  pl.BlockSpec((1, d), lambda i, kk: (0, 0)),
                  pl.BlockSpec((1, d), lambda i, kk: (0, 0))],
        out_specs=[pl.BlockSpec((tm, d), lambda i, kk: (i, 0)),
                   pl.BlockSpec((tm, d), lambda i, kk: (i, 0))],
        out_shape=[jax.ShapeDtypeStruct((m, d), F32), jax.ShapeDtypeStruct((m, d), BF16)],
        scratch_shapes=[pltpu.VMEM((tm, d), F32)],
        compiler_params=_cparams("parallel", "arbitrary"),
    )(x, w, res, g.reshape(1, d), b.reshape(1, d))


def _gla_gate_kernel(x_ref, wgl_ref, wup_ref, b_ref, o_ref):
    gl = jnp.dot(x_ref[...], wgl_ref[...], preferred_element_type=F32)
    z = _dot(gl, wup_ref[...]) + b_ref[...]
    o_ref[...] = (jnp.minimum(z, 0.0) - jnp.log(1.0 + jnp.exp(-jnp.abs(z)))) * (1.0 / GLA_GATE_NORM)


def _gla_gate(x, w_gl, w_up, b_gk):
    m, k = x.shape
    tm = 512
    wgl = jnp.zeros((k, LANES), BF16).at[:, :GLA_GATE_RANK].set(w_gl.astype(BF16))
    wup = jnp.zeros((LANES, GLA_QK), BF16).at[:GLA_GATE_RANK].set(w_up.astype(BF16))
    return pl.pallas_call(
        _gla_gate_kernel,
        grid=(m // tm,),
        in_specs=[pl.BlockSpec((tm, k), lambda i: (i, 0)),
                  pl.BlockSpec((k, LANES), lambda i: (0, 0)),
                  pl.BlockSpec((LANES, GLA_QK), lambda i: (0, 0)),
                  pl.BlockSpec((1, GLA_QK), lambda i: (0, 0))],
        out_specs=pl.BlockSpec((tm, GLA_QK), lambda i: (i, 0)),
        out_shape=jax.ShapeDtypeStruct((m, GLA_QK), F32),
        compiler_params=_cparams("parallel"),
    )(x, wgl, wup, b_gk.reshape(1, GLA_QK).astype(F32))


def _gdn_gate_kernel(x_ref, w_ref, alog_ref, dtb_ref, o_ref):
    p = jnp.dot(x_ref[...], w_ref[...], preferred_element_type=F32)
    lane = lax.broadcasted_iota(jnp.int32, p.shape, 1)
    beta = _sigmoid(p)
    g = -jnp.exp(alog_ref[...]) * _softplus(p + dtb_ref[...])
    o_ref[...] = jnp.where(lane < GDN_V_HEADS, beta, jnp.where(lane < 2 * GDN_V_HEADS, g, 0.0))


def _gdn_gate(x, w_ba, a_log, dt_bias):
    m, k = x.shape
    tm = 512
    hv = GDN_V_HEADS
    w = jnp.zeros((k, LANES), BF16).at[:, :2 * hv].set(w_ba.astype(BF16))
    alog = jnp.zeros((1, LANES), F32).at[0, hv:2 * hv].set(a_log.astype(F32))
    dtb = jnp.zeros((1, LANES), F32).at[0, hv:2 * hv].set(dt_bias.astype(F32))
    return pl.pallas_call(
        _gdn_gate_kernel,
        grid=(m // tm,),
        in_specs=[pl.BlockSpec((tm, k), lambda i: (i, 0)),
                  pl.BlockSpec((k, LANES), lambda i: (0, 0)),
                  pl.BlockSpec((1, LANES), lambda i: (0, 0)),
                  pl.BlockSpec((1, LANES), lambda i: (0, 0))],
        out_specs=pl.BlockSpec((tm, LANES), lambda i: (i, 0)),
        out_shape=jax.ShapeDtypeStruct((m, LANES), F32),
        compiler_params=_cparams("parallel"),
    )(x, w, alog, dtb)


def _router_kernel(x_ref, w_ref, b_ref, o_ref):
    logits = _dot(x_ref[...], w_ref[...]) + b_ref[...]
    lane = lax.broadcasted_iota(jnp.int32, logits.shape, 1)
    neg = -jnp.inf
    logits = jnp.where(lane < N_EXPERTS, logits, neg)
    m1 = jnp.max(logits, axis=1, keepdims=True)
    i1 = jnp.min(jnp.where(logits == m1, lane, LANES), axis=1, keepdims=True)
    rest = jnp.where(lane == i1, neg, logits)
    m2 = jnp.max(rest, axis=1, keepdims=True)
    i2 = jnp.min(jnp.where(rest == m2, lane, LANES), axis=1, keepdims=True)
    e = jnp.exp(m2 - m1)
    denom = 1.0 + e
    sel = jnp.where(lane == 0, i1.astype(F32), jnp.where(lane == 1, i2.astype(F32), 0.0))
    o_ref[...] = jnp.where(lane == 2, 1.0 / denom, jnp.where(lane == 3, e / denom, sel))


def _router(x, router_w, router_b):
    m, k = x.shape
    tm = 512
    w = jnp.zeros((k, LANES), BF16).at[:, :N_EXPERTS].set(router_w.astype(BF16))
    b = jnp.zeros((1, LANES), F32).at[0, :N_EXPERTS].set(router_b.astype(F32))
    return pl.pallas_call(
        _router_kernel,
        grid=(m // tm,),
        in_specs=[pl.BlockSpec((tm, k), lambda i: (i, 0)),
                  pl.BlockSpec((k, LANES), lambda i: (0, 0)),
                  pl.BlockSpec((1, LANES), lambda i: (0, 0))],
        out_specs=pl.BlockSpec((tm, LANES), lambda i: (i, 0)),
        out_shape=jax.ShapeDtypeStruct((m, LANES), F32),
        compiler_params=_cparams("parallel"),
    )(x, w, b)


def _row_copy(src_hbm, dst, sem, src_row, dst_row):
    return pltpu.make_async_copy(src_hbm.at[pl.ds(src_row, 1)], dst.at[pl.ds(dst_row, 1)], sem)


def _gather_rows(src_hbm, idx_ref, dst, sem, n):
    def start(r, carry):
        _row_copy(src_hbm, dst, sem, idx_ref[0, r], r).start()
        return carry

    def wait(r, carry):
        _row_copy(src_hbm, dst, sem, idx_ref[0, r], r).wait()
        return carry

    lax.fori_loop(0, n, start, 0)
    lax.fori_loop(0, n, wait, 0)


def _moe_gather_kernel(idx_ref, x_hbm, o_ref, buf_ref, sem, *, tg):
    _gather_rows(x_hbm, idx_ref, buf_ref, sem, tg)
    o_ref[...] = buf_ref[...].astype(o_ref.dtype)


def _moe_gather(x, src, tg):
    d = x.shape[1]
    n = src.shape[0]
    return pl.pallas_call(
        functools.partial(_moe_gather_kernel, tg=tg),
        grid=(n // tg,),
        in_specs=[pl.BlockSpec((None, 1, tg), lambda i: (i, 0, 0), memory_space=pltpu.SMEM),
                  pl.BlockSpec(memory_space=pl.ANY)],
        out_specs=pl.BlockSpec((tg, d), lambda i: (i, 0)),
        out_shape=jax.ShapeDtypeStruct((n, d), BF16),
        scratch_shapes=[pltpu.VMEM((tg, d), F32), pltpu.SemaphoreType.DMA(())],
        compiler_params=_cparams("arbitrary"),
    )(src.reshape(n // tg, 1, tg), x)


def _moe_up_kernel(te_ref, na_ref, x_ref, wg_ref, wu_ref, o_ref):
    i = pl.program_id(1)

    @pl.when(i < na_ref[0])
    def _():
        x = x_ref[...]
        a = jnp.dot(x, wg_ref[...], preferred_element_type=F32)
        b = jnp.dot(x, wu_ref[...], preferred_element_type=F32)
        o_ref[...] = (_silu(a) * b).astype(o_ref.dtype)

    @pl.when(i >= na_ref[0])
    def _():
        o_ref[...] = jnp.zeros_like(o_ref)


def _moe_up(xg, w_gate, w_up, tile_e, n_active, tm):
    n, k = xg.shape
    f = w_gate.shape[2]
    tn = _pick_tile(f, (1024, 512))
    return pl.pallas_call(
        _moe_up_kernel,
        grid_spec=pltpu.PrefetchScalarGridSpec(
            num_scalar_prefetch=2,
            grid=(f // tn, n // tm),
            in_specs=[pl.BlockSpec((tm, k), lambda j, i, te, na: (i, 0)),
                      pl.BlockSpec((None, k, tn), lambda j, i, te, na: (te[i], 0, j)),
                      pl.BlockSpec((None, k, tn), lambda j, i, te, na: (te[i], 0, j))],
            out_specs=pl.BlockSpec((tm, tn), lambda j, i, te, na: (i, j))),
        out_shape=jax.ShapeDtypeStruct((n, f), BF16),
        compiler_params=_cparams("parallel", "arbitrary"),
    )(tile_e, n_active, xg, w_gate, w_up)


def _moe_down_kernel(te_ref, na_ref, h_ref, w_ref, o_ref):
    i = pl.program_id(1)

    @pl.when(i < na_ref[0])
    def _():
        o_ref[...] = jnp.dot(h_ref[...], w_ref[...], preferred_element_type=F32)

    @pl.when(i >= na_ref[0])
    def _():
        o_ref[...] = jnp.zeros_like(o_ref)


def _moe_down(h, w_down, tile_e, n_active, tm):
    n, f = h.shape
    d = w_down.shape[2]
    tn = 512
    return pl.pallas_call(
        _moe_down_kernel,
        grid_spec=pltpu.PrefetchScalarGridSpec(
            num_scalar_prefetch=2,
            grid=(d // tn, n // tm),
            in_specs=[pl.BlockSpec((tm, f), lambda j, i, te, na: (i, 0)),
                      pl.BlockSpec((None, f, tn), lambda j, i, te, na: (te[i], 0, j))],
            out_specs=pl.BlockSpec((tm, tn), lambda j, i, te, na: (i, j))),
        out_shape=jax.ShapeDtypeStruct((n, d), F32),
        compiler_params=_cparams("parallel", "arbitrary"),
    )(tile_e, n_active, h, w_down)


def _moe_combine_kernel(p1_ref, p2_ref, ys_hbm, sel_ref, res_ref, g_ref, b_ref, o_ref, buf_ref, sem, *, tt):
    _gather_rows(ys_hbm, p1_ref, buf_ref.at[0], sem.at[0], tt)
    _gather_rows(ys_hbm, p2_ref, buf_ref.at[1], sem.at[1], tt)
    sel = sel_ref[...]
    lane = lax.broadcasted_iota(jnp.int32, sel.shape, 1)
    w1 = jnp.sum(jnp.where(lane == 2, sel, 0.0), axis=1, keepdims=True)
    w2 = jnp.sum(jnp.where(lane == 3, sel, 0.0), axis=1, keepdims=True)
    ff = buf_ref[0] * w1 + buf_ref[1] * w2
    o_ref[...] = _layer_norm(ALPHA * res_ref[...] + ff, g_ref[...], b_ref[...])


def _moe_combine(ys, pos1, pos2, sel, res, g, b):
    m, d = res.shape
    tt = 256
    idx_spec = pl.BlockSpec((None, 1, tt), lambda i: (i, 0, 0), memory_space=pltpu.SMEM)
    return pl.pallas_call(
        functools.partial(_moe_combine_kernel, tt=tt),
        grid=(m // tt,),
        in_specs=[idx_spec, idx_spec,
                  pl.BlockSpec(memory_space=pl.ANY),
                  pl.BlockSpec((tt, LANES), lambda i: (i, 0)),
                  pl.BlockSpec((tt, d), lambda i: (i, 0)),
                  pl.BlockSpec((1, d), lambda i: (0, 0)),
                  pl.BlockSpec((1, d), lambda i: (0, 0))],
        out_specs=pl.BlockSpec((tt, d), lambda i: (i, 0)),
        out_shape=jax.ShapeDtypeStruct((m, d), F32),
        scratch_shapes=[pltpu.VMEM((2, tt, d), F32), pltpu.SemaphoreType.DMA((2,))],
        compiler_params=_cparams("arbitrary"),
    )(pos1.reshape(m // tt, 1, tt), pos2.reshape(m // tt, 1, tt), ys, sel, res, g.reshape(1, d), b.reshape(1, d))


def _moe_plan(sel, tm):
    m = sel.shape[0]
    i1 = sel[:, 0].astype(jnp.int32)
    i2 = sel[:, 1].astype(jnp.int32)
    e = jnp.concatenate([i1, i2])
    onehot = (e[:, None] == jnp.arange(N_EXPERTS, dtype=jnp.int32)[None, :]).astype(jnp.int32)
    csum = jnp.cumsum(onehot, axis=0)
    rank = jnp.sum(csum * onehot, axis=1) - 1
    counts = csum[-1]
    padded = ((counts + tm - 1) // tm) * tm
    ends = jnp.cumsum(padded)
    starts = ends - padded
    pos = jnp.sum(starts[None, :] * onehot, axis=1) + rank
    n_slots = 2 * m + N_EXPERTS * tm
    tok = jnp.arange(m, dtype=jnp.int32)
    src = jnp.zeros((n_slots,), jnp.int32).at[pos].set(jnp.concatenate([tok, tok]))
    tile_start = jnp.arange(n_slots // tm, dtype=jnp.int32) * tm
    tile_e = jnp.sum((tile_start[:, None] >= ends[None, :]).astype(jnp.int32), axis=1)
    n_active = (ends[-1] // tm).astype(jnp.int32).reshape(1)
    return src, pos[:m], pos[m:], jnp.minimum(tile_e, N_EXPERTS - 1), n_active


def _moe(x, p, g, b):
    tm = 512
    sel = _router(x, p["router_w"], p["router_b"])
    src, pos1, pos2, tile_e, n_active = _moe_plan(sel, tm)
    xg = _moe_gather(x, src, tm)
    hdn = _moe_up(xg, p["moe_w_gate"], p["moe_w_up"], tile_e, n_active, tm)
    ys = _moe_down(hdn, p["moe_w_down"], tile_e, n_active, tm)
    return _moe_combine(ys, pos1, pos2, sel, x, g, b)


def _gla_kernel(q_ref, k_ref, v_ref, la_ref, r_ref, s0_ref, ng_ref, o_ref, sout_ref, st_ref, bc_ref,
                *, tt, nt):
    sub = GLA_SUB
    i = pl.program_id(2)

    @pl.when(i == 0)
    def _():
        st_ref[...] = s0_ref[...].T

    rows = lax.broadcasted_iota(jnp.int32, (tt, tt), 0)
    cols = lax.broadcasted_iota(jnp.int32, (tt, tt), 1)
    tri = jnp.where(((rows & -sub) == (cols & -sub)) & (cols <= rows), 1.0, 0.0).astype(F32)
    bc_ref[...] = jnp.dot(tri, la_ref[...], precision=lax.Precision.HIGHEST, preferred_element_type=F32)
    trow = lax.broadcasted_iota(jnp.int32, (sub, GLA_DK), 0)
    lane = lax.broadcasted_iota(jnp.int32, (sub, LANES), 1)
    ng = ng_ref[...]

    def chunk(c, carry):
        sl = pl.ds(pl.multiple_of(c * sub, sub), sub)
        b = bc_ref[sl, :]
        q = q_ref[sl, :] * (GLA_DK ** -0.5)
        k = k_ref[sl, :]
        v = v_ref[sl, :]
        st = st_ref[...]
        o = _dot_nt(q * jnp.exp(b), st)
        att = jnp.zeros((sub, LANES), F32)
        for s in range(sub):
            e = jnp.exp(jnp.where(trow >= s, b - b[s:s + 1, :], -jnp.inf))
            col = jnp.sum(q * k[s:s + 1, :] * e, axis=1, keepdims=True)
            att = jnp.where(lane == s, col, att)
        o = o + _dot(att[:, :sub], v)
        b_last = b[sub - 1:sub, :]
        st_ref[...] = st * jnp.exp(b_last) + _dot_tn(v, k * jnp.exp(b_last - b))
        ms = jnp.mean(o * o, axis=1, keepdims=True)
        o_ref[sl, :] = (o * lax.rsqrt(ms + LN_EPS) * ng * _silu(r_ref[sl, :])).astype(o_ref.dtype)
        return carry

    lax.fori_loop(0, tt // sub, chunk, 0)

    @pl.when(i == nt - 1)
    def _():
        sout_ref[...] = st_ref[...].T


def _gla(qkvr, log_a, s0, norm_g, bsz, t):
    tt = _pick_tile(t, (128, 16))
    nt = t // tt
    h = GLA_HEADS
    vb = GLA_QK * 2 // GLA_DV
    rb = vb + GLA_V // GLA_DV
    row = lambda b, hh, i: b * nt + i
    return pl.pallas_call(
        functools.partial(_gla_kernel, tt=tt, nt=nt),
        grid=(bsz, h, nt),
        in_specs=[pl.BlockSpec((tt, GLA_DK), lambda b, hh, i: (row(b, hh, i), hh)),
                  pl.BlockSpec((tt, GLA_DK), lambda b, hh, i: (row(b, hh, i), h + hh)),
                  pl.BlockSpec((tt, GLA_DV), lambda b, hh, i: (row(b, hh, i), vb + hh)),
                  pl.BlockSpec((tt, GLA_DK), lambda b, hh, i: (row(b, hh, i), hh)),
                  pl.BlockSpec((tt, GLA_DV), lambda b, hh, i: (row(b, hh, i), rb + hh)),
                  pl.BlockSpec((None, None, GLA_DK, GLA_DV), lambda b, hh, i: (b, hh, 0, 0)),
                  pl.BlockSpec((1, GLA_DV), lambda b, hh, i: (0, 0))],
        out_specs=[pl.BlockSpec((tt, GLA_DV), lambda b, hh, i: (row(b, hh, i), hh)),
                   pl.BlockSpec((None, None, GLA_DK, GLA_DV), lambda b, hh, i: (b, hh, 0, 0))],
        out_shape=[jax.ShapeDtypeStruct((bsz * t, GLA_V), BF16),
                   jax.ShapeDtypeStruct((bsz, h, GLA_DK, GLA_DV), F32)],
        scratch_shapes=[pltpu.VMEM((GLA_DV, GLA_DK), F32), pltpu.VMEM((tt, GLA_DK), F32)],
        compiler_params=_cparams("parallel", "parallel", "arbitrary"),
    )(qkvr, qkvr, qkvr, log_a, qkvr, s0, norm_g.reshape(1, GLA_DV).astype(F32))


def _conf_conv_kernel(u_ref, buf_ref, w_ref, dwb_ref, g_ref, b_ref, c_ref, bufo_ref, ext_ref, *, tt, nt):
    hist = CONV_W - 1
    pad = 32 - hist
    i = pl.program_id(1)

    @pl.when(i == 0)
    def _():
        ext_ref[0:pad, :] = jnp.zeros((pad, CONV_CH), F32)
        ext_ref[pad:32, :] = buf_ref[...]

    ext_ref[32:32 + tt, :] = u_ref[...]
    cw = 256
    for cb in range(CONV_CH // cw):
        cs = slice(cb * cw, (cb + 1) * cw)
        acc = jnp.zeros((tt, cw), F32)
        for j in range(CONV_W):
            acc = acc + w_ref[j:j + 1, cs] * ext_ref[pad + j:pad + j + tt, cs]
        c_ref[:, cs] = (acc + dwb_ref[:, cs]).astype(c_ref.dtype)
    y = _layer_norm(c_ref[...].astype(F32), g_ref[...], b_ref[...])
    c_ref[...] = _silu(y).astype(c_ref.dtype)

    @pl.when(i == nt - 1)
    def _():
        bufo_ref[...] = ext_ref[tt + pad:tt + 32, :]

    ext_ref[0:32, :] = ext_ref[tt:tt + 32, :]


def _conf_conv(u, buf, dw_w, dw_b, ln_g, ln_b, bsz, t):
    tt = _pick_tile(t, (128, 16))
    nt = t // tt
    vec = lambda a: a.reshape(1, CONV_CH).astype(F32)
    return pl.pallas_call(
        functools.partial(_conf_conv_kernel, tt=tt, nt=nt),
        grid=(bsz, nt),
        in_specs=[pl.BlockSpec((tt, CONV_CH), lambda b, i: (b * nt + i, 0)),
                  pl.BlockSpec((None, CONV_W - 1, CONV_CH), lambda b, i: (b, 0, 0)),
                  pl.BlockSpec((CONV_W, CONV_CH), lambda b, i: (0, 0)),
                  pl.BlockSpec((1, CONV_CH), lambda b, i: (0, 0)),
                  pl.BlockSpec((1, CONV_CH), lambda b, i: (0, 0)),
                  pl.BlockSpec((1, CONV_CH), lambda b, i: (0, 0))],
        out_specs=[pl.BlockSpec((tt, CONV_CH), lambda b, i: (b * nt + i, 0)),
                   pl.BlockSpec((None, CONV_W - 1, CONV_CH), lambda b, i: (b, 0, 0))],
        out_shape=[jax.ShapeDtypeStruct((bsz * t, CONV_CH), F32),
                   jax.ShapeDtypeStruct((bsz, CONV_W - 1, CONV_CH), F32)],
        scratch_shapes=[pltpu.VMEM((32 + tt, CONV_CH), F32)],
        compiler_params=_cparams("parallel", "arbitrary"),
    )(u, buf, dw_w.astype(F32), vec(dw_b), vec(ln_g), vec(ln_b))


def _gdn_conv_kernel(x_ref, buf_ref, w_ref, o_ref, bufo_ref, ext_ref, *, tt, nt, tc):
    hist = GDN_CONV_W - 1
    pad = 8 - hist
    cb = pl.program_id(1)
    i = pl.program_id(2)

    @pl.when(i == 0)
    def _():
        ext_ref[0:pad, :] = jnp.zeros((pad, tc), F32)
        ext_ref[pad:8, :] = buf_ref[...]

    ext_ref[8:8 + tt, :] = x_ref[...]
    acc = jnp.zeros((tt, tc), F32)
    for j in range(GDN_CONV_W):
        acc = acc + w_ref[j:j + 1, :] * ext_ref[pad + j:pad + j + tt, :]
    y = _silu(acc)
    is_qk = cb < (2 * GDN_Q) // tc
    scale = jnp.where(cb < GDN_Q // tc, GDN_DK ** -0.5, 1.0)
    for hh in range(tc // GDN_DK):
        cs = slice(hh * GDN_DK, (hh + 1) * GDN_DK)
        seg = y[:, cs]
        nrm = seg * lax.rsqrt(jnp.sum(seg * seg, axis=1, keepdims=True) + L2_EPS) * scale
        o_ref[:, cs] = jnp.where(is_qk, nrm, seg)

    @pl.when(i == nt - 1)
    def _():
        bufo_ref[...] = ext_ref[tt + pad:tt + 8, :]

    ext_ref[0:8, :] = ext_ref[tt:tt + 8, :]


def _gdn_conv(proj, buf, conv_w, bsz, t):
    tt = _pick_tile(t, (256, 16))
    nt = t // tt
    tc = 1024
    return pl.pallas_call(
        functools.partial(_gdn_conv_kernel, tt=tt, nt=nt, tc=tc),
        grid=(bsz, GDN_CONV_DIM // tc, nt),
        in_specs=[pl.BlockSpec((tt, tc), lambda b, c, i: (b * nt + i, c)),
                  pl.BlockSpec((None, GDN_CONV_W - 1, tc), lambda b, c, i: (b, 0, c)),
                  pl.BlockSpec((GDN_CONV_W, tc), lambda b, c, i: (0, c))],
        out_specs=[pl.BlockSpec((tt, tc), lambda b, c, i: (b * nt + i, c)),
                   pl.BlockSpec((None, GDN_CONV_W - 1, tc), lambda b, c, i: (b, 0, c))],
        out_shape=[jax.ShapeDtypeStruct((bsz * t, GDN_CONV_DIM), F32),
                   jax.ShapeDtypeStruct((bsz, GDN_CONV_W - 1, GDN_CONV_DIM), F32)],
        scratch_shapes=[pltpu.VMEM((8 + tt, tc), F32)],
        compiler_params=_cparams("parallel", "parallel", "arbitrary"),
    )(proj, buf, conv_w.astype(F32))


def _unit_lower_inverse(m, size):
    rows = lax.broadcasted_iota(jnp.int32, (size, size), 0)
    cols = lax.broadcasted_iota(jnp.int32, (size, size), 1)
    acc = jnp.where(rows == cols, 1.0, 0.0) - m
    power = _dot(m, m)
    terms = 2
    while terms < size:
        new_acc = acc + _dot(power, acc)
        if 2 * terms < size:
            power = _dot(power, power)
        acc = new_acc
        terms *= 2
    return acc


def _gdn_kernel(q_ref, k_ref, v_ref, z_ref, gb_ref, s0_ref, ng_ref, o_ref, sout_ref, s_ref,
                *, chunk, tt, nt):
    hq = pl.program_id(1)
    i = pl.program_id(2)
    rep = GDN_V_HEADS // GDN_QK_HEADS

    @pl.when(i == 0)
    def _():
        s_ref[...] = s0_ref[...]

    lane = lax.broadcasted_iota(jnp.int32, (chunk, LANES), 1)
    rows = lax.broadcasted_iota(jnp.int32, (chunk, chunk), 0)
    cols = lax.broadcasted_iota(jnp.int32, (chunk, chunk), 1)
    incl = cols <= rows
    strict = cols < rows
    ng = ng_ref[...]

    def body(c, carry):
        sl = pl.ds(pl.multiple_of(c * chunk, chunk), chunk)
        q = q_ref[sl, :]
        k = k_ref[sl, :]
        gb = gb_ref[sl, :]
        kk = _dot_nt(k, k)
        qk = _dot_nt(q, k)
        for j in range(rep):
            hv = rep * hq + j
            vs = slice(j * GDN_DV, (j + 1) * GDN_DV)
            beta = jnp.sum(jnp.where(lane == hv, gb, 0.0), axis=1, keepdims=True)
            g = jnp.sum(jnp.where(lane == GDN_V_HEADS + hv, gb, 0.0), axis=1, keepdims=True)
            g_row = jnp.sum(jnp.where(rows == cols, g, 0.0), axis=0, keepdims=True)
            gcum = jnp.sum(jnp.where(incl, g_row, 0.0), axis=1, keepdims=True)
            gcum_row = jnp.sum(jnp.where(rows <= cols, g, 0.0), axis=0, keepdims=True)
            rel = jnp.exp(jnp.where(incl, gcum - gcum_row, -jnp.inf))
            tinv = _unit_lower_inverse(jnp.where(strict, kk * beta * rel, 0.0), chunk)
            eg = jnp.exp(gcum)
            v = v_ref[sl, vs]
            u = _dot(tinv, v * beta)
            w = _dot(tinv, k * (beta * eg))
            s = s_ref[j]
            v_new = u - _dot(w, s)
            o = _dot(q * eg, s) + _dot(qk * rel, v_new)
            g_last = gcum[chunk - 1:chunk, :]
            s_ref[j] = s * jnp.exp(g_last) + _dot_tn(k * jnp.exp(g_last - gcum), v_new)
            ms = jnp.mean(o * o, axis=1, keepdims=True)
            o_ref[sl, vs] = (o * lax.rsqrt(ms + LN_EPS) * ng * _silu(z_ref[sl, vs])).astype(o_ref.dtype)
        return carry

    lax.fori_loop(0, tt // chunk, body, 0)

    @pl.when(i == nt - 1)
    def _():
        sout_ref[...] = s_ref[...]


def _gdn(qkv, proj, gb, s0, norm_g, bsz, t):
    chunk = min(CHUNK, t)
    tt = _pick_tile(t, (256, chunk))
    nt = t // tt
    rep = GDN_V_HEADS // GDN_QK_HEADS
    vw = rep * GDN_DV
    hq_n = GDN_QK_HEADS
    row = lambda b, i: b * nt + i
    return pl.pallas_call(
        functools.partial(_gdn_kernel, chunk=chunk, tt=tt, nt=nt),
        grid=(bsz, hq_n, nt),
        in_specs=[pl.BlockSpec((tt, GDN_DK), lambda b, h, i: (row(b, i), h)),
                  pl.BlockSpec((tt, GDN_DK), lambda b, h, i: (row(b, i), hq_n + h)),
                  pl.BlockSpec((tt, vw), lambda b, h, i: (row(b, i), 2 * GDN_Q // vw + h)),
                  pl.BlockSpec((tt, vw), lambda b, h, i: (row(b, i), GDN_CONV_DIM // vw + h)),
                  pl.BlockSpec((tt, LANES), lambda b, h, i: (row(b, i), 0)),
                  pl.BlockSpec((None, rep, GDN_DK, GDN_DV), lambda b, h, i: (b, h, 0, 0)),
                  pl.BlockSpec((1, GDN_DV), lambda b, h, i: (0, 0))],
        out_specs=[pl.BlockSpec((tt, vw), lambda b, h, i: (row(b, i), h)),
                   pl.BlockSpec((None, rep, GDN_DK, GDN_DV), lambda b, h, i: (b, h, 0, 0))],
        out_shape=[jax.ShapeDtypeStruct((bsz * t, GDN_V), BF16),
                   jax.ShapeDtypeStruct((bsz, GDN_V_HEADS, GDN_DK, GDN_DV), F32)],
        scratch_shapes=[pltpu.VMEM((rep, GDN_DK, GDN_DV), F32)],
        compiler_params=_cparams("parallel", "parallel", "arbitrary"),
    )(qkv, qkv, qkv, proj, gb, s0, norm_g.reshape(1, GDN_DV).astype(F32))


def _mem_attn_kernel(q_ref, k_ref, v_ref, o_ref):
    for h in range(MEM_HEADS):
        hs = slice(h * MEM_HD, (h + 1) * MEM_HD)
        s = _dot_nt(q_ref[:, hs], k_ref[:, hs]) * (MEM_HD ** -0.5)
        s = s - jnp.max(s, axis=1, keepdims=True)
        p = jnp.exp(s)
        p = p / jnp.sum(p, axis=1, keepdims=True)
        o_ref[:, hs] = _dot(p, v_ref[:, hs]).astype(o_ref.dtype)


def _mem_attn(q, mk, mv, bsz, t):
    tq = _pick_tile(t, (512, 16))
    nt = t // tq
    return pl.pallas_call(
        _mem_attn_kernel,
        grid=(bsz, nt),
        in_specs=[pl.BlockSpec((tq, D_MODEL), lambda b, i: (b * nt + i, 0)),
                  pl.BlockSpec((None, MEM_LEN, D_MODEL), lambda b, i: (b, 0, 0)),
                  pl.BlockSpec((None, MEM_LEN, D_MODEL), lambda b, i: (b, 0, 0))],
        out_specs=pl.BlockSpec((tq, D_MODEL), lambda b, i: (b * nt + i, 0)),
        out_shape=jax.ShapeDtypeStruct((bsz * t, D_MODEL), BF16),
        compiler_params=_cparams("parallel", "arbitrary"),
    )(q, mk, mv)


def _trunk(x, bsz, t, gla_s, conf_buf, gdn_s, gdn_buf, mem_k, mem_v, p):
    m = bsz * t
    xb = x.astype(BF16)

    qkvr = _matmul(xb, p["w_qkvr"], F32)[0]
    log_a = _gla_gate(xb, p["w_gl"], p["w_gk_up"], p["b_gk"])
    u = _mm_gated(xb, p["w_glu"], p["w_glu"], CONV_CH, CONV_CH, "glu", F32)
    o_a, gla_new = _gla(qkvr, log_a, gla_s, p["gla_norm_g"], bsz, t)
    c, conf_new = _conf_conv(u, conf_buf, p["conv_dw_w"], p["conv_dw_b"], p["conv_ln_g"], p["conv_ln_b"], bsz, t)
    mix = jnp.concatenate([o_a, c.astype(BF16)], axis=1)
    x, xb = _mm_res_ln(mix, p["w_out_ab"], x, p["ln_mix_g"][0], p["ln_mix_b"][0])
    qm = _matmul(xb, p["w_mq"][0:1], BF16)[0]
    att = _mem_attn(qm, mem_k[0], mem_v[0], bsz, t)
    x, xb = _mm_res_ln(att, p["w_mo"][0], x, p["ln_mem_g"][0], p["ln_mem_b"][0])
    hdn = _mm_gated(xb, p["ff_w_gate"], p["ff_w_up"], p["ff_w_gate"].shape[2], 0, "swiglu", BF16)
    x, xb = _mm_res_ln(hdn, p["ff_w_down"], x, p["ln_ff_g"][0], p["ln_ff_b"][0])

    proj = _matmul(xb, p["w_qkvz"], F32)[0]
    gb = _gdn_gate(xb, p["w_ba"], p["gdn_a_log"], p["gdn_dt_bias"])
    qkv, gdnc_new = _gdn_conv(proj, gdn_buf, p["gdn_conv_w"], bsz, t)
    o_c, gdn_new = _gdn(qkv, proj, gb, gdn_s, p["gdn_norm_g"], bsz, t)
    x, xb = _mm_res_ln(o_c, p["w_out_c"], x, p["ln_mix_g"][1], p["ln_mix_b"][1])
    qm = _matmul(xb, p["w_mq"][1:2], BF16)[0]
    att = _mem_attn(qm, mem_k[1], mem_v[1], bsz, t)
    x, xb = _mm_res_ln(att, p["w_mo"][1], x, p["ln_mem_g"][1], p["ln_mem_b"][1])
    return x, gla_new, conf_new, gdn_new, gdnc_new


def kernel(x_prompt, x_sample, mem_prompt, state_gla, state_conf_conv, state_gdn, state_gdn_conv, cache_mem_k, cache_mem_v, ln_mix_g, ln_mix_b, ln_mem_g, ln_mem_b, ln_ff_g, ln_ff_b, w_mq, w_mk, w_mv, w_mo, w_in_ab, w_gk_up, b_gk, gla_norm_g, conv_dw_w, conv_dw_b, conv_ln_g, conv_ln_b, w_out_ab, ff_w_gate, ff_w_up, ff_w_down, w_in_c, gdn_conv_w, gdn_a_log, gdn_dt_bias, gdn_norm_g, w_out_c, router_w, router_b, moe_w_gate, moe_w_up, moe_w_down):
    bf = lambda a: a.astype(BF16)
    qkvr_n = 2 * GLA_QK + 2 * GLA_V
    glu_0 = qkvr_n + GLA_GATE_RANK
    qkvz_n = GDN_CONV_DIM + GDN_V
    p = dict(
        ln_mix_g=ln_mix_g, ln_mix_b=ln_mix_b, ln_mem_g=ln_mem_g, ln_mem_b=ln_mem_b, ln_ff_g=ln_ff_g, ln_ff_b=ln_ff_b,
        w_mq=bf(w_mq), w_mo=bf(w_mo),
        w_qkvr=bf(w_in_ab[:, :, :qkvr_n]), w_gl=w_in_ab[0, :, qkvr_n:glu_0], w_glu=bf(w_in_ab[:, :, glu_0:]),
        w_gk_up=w_gk_up[0], b_gk=b_gk[0], gla_norm_g=gla_norm_g[0],
        conv_dw_w=conv_dw_w[0], conv_dw_b=conv_dw_b[0], conv_ln_g=conv_ln_g[0], conv_ln_b=conv_ln_b[0],
        w_out_ab=bf(w_out_ab[0]), ff_w_gate=bf(ff_w_gate), ff_w_up=bf(ff_w_up), ff_w_down=bf(ff_w_down[0]),
        w_qkvz=bf(w_in_c[:, :, :qkvz_n]), w_ba=w_in_c[0, :, qkvz_n:], gdn_conv_w=gdn_conv_w[0],
        gdn_a_log=gdn_a_log[0], gdn_dt_bias=gdn_dt_bias[0], gdn_norm_g=gdn_norm_g[0], w_out_c=bf(w_out_c[0]),
        router_w=router_w[0], router_b=router_b[0],
        moe_w_gate=bf(moe_w_gate[0]), moe_w_up=bf(moe_w_up[0]),
        moe_w_down=bf(moe_w_down[0]),
    )
    nb, nt, d = x_prompt.shape
    sb, st, _ = x_sample.shape
    dt = x_prompt.dtype

    mem = bf(mem_prompt.reshape(nb * MEM_LEN, d))
    mem_k_p = _matmul(mem, bf(w_mk), F32)
    mem_v_p = _matmul(mem, bf(w_mv), F32)
    y_p, gla_p, conf_p, gdn_p, gdnc_p = _trunk(
        x_prompt.reshape(nb * nt, d), nb, nt,
        jnp.zeros((nb, GLA_HEADS, GLA_DK, GLA_DV), dt), jnp.zeros((nb, CONV_W - 1, CONV_CH), dt),
        jnp.zeros((nb, GDN_V_HEADS, GDN_DK, GDN_DV), dt), jnp.zeros((nb, GDN_CONV_W - 1, GDN_CONV_DIM), dt),
        mem_k_p.reshape(DEPTH, nb, MEM_LEN, d), mem_v_p.reshape(DEPTH, nb, MEM_LEN, d), p)

    y_s, gla_s, conf_s, gdn_s, gdnc_s = _trunk(
        x_sample.reshape(sb * st, d), sb, st, state_gla[0], state_conf_conv[0], state_gdn[0], state_gdn_conv[0],
        cache_mem_k.reshape(DEPTH, sb, MEM_LEN, d), cache_mem_v.reshape(DEPTH, sb, MEM_LEN, d), p)

    y = _moe(jnp.concatenate([y_p, y_s], axis=0), p, ln_ff_g[1], ln_ff_b[1])
    y_p, y_s = y[:nb * nt], y[nb * nt:]

    kv_shape = (DEPTH, nb, MEM_LEN, MEM_HEADS, MEM_HD)
    return (y_p.reshape(nb, nt, d), y_s.reshape(sb, st, d),
            gla_p[None], conf_p[None], gdn_p[None], gdnc_p[None],
            mem_k_p.reshape(kv_shape), mem_v_p.reshape(kv_shape),
            gla_s[None], conf_s[None], gdn_s[None], gdnc_s[None])
```

```python
import functools

import jax
import jax.numpy as jnp
from jax import lax
from jax.experimental import pallas as pl
from jax.experimental.pallas import tpu as pltpu

F32 = jnp.float32
BF16 = jnp.bfloat16

D_MODEL = 2048
DEPTH = 2
ALPHA = (2 * DEPTH) ** 0.25
LN_EPS = 1e-5
L2_EPS = 1e-6
CHUNK = 64

GLA_HEADS = 4
GLA_DK = 128
GLA_DV = 256
GLA_QK = GLA_HEADS * GLA_DK
GLA_V = GLA_HEADS * GLA_DV
GLA_GATE_RANK = 16
GLA_GATE_NORM = 16.0
GLA_SUB = 16
CONV_CH = 1024
CONV_W = 31
GDN_QK_HEADS = 16
GDN_V_HEADS = 32
GDN_DK = 128
GDN_DV = 128
GDN_CONV_W = 4
GDN_Q = GDN_QK_HEADS * GDN_DK
GDN_V = GDN_V_HEADS * GDN_DV
GDN_CONV_DIM = 2 * GDN_Q + GDN_V
MEM_LEN = 256
MEM_HEADS = 4
MEM_HD = D_MODEL // MEM_HEADS
N_EXPERTS = 8
LANES = 128
VMEM_LIMIT_BYTES = 56 * 1024 * 1024


def _cparams(*sem):
    return pltpu.CompilerParams(dimension_semantics=sem, vmem_limit_bytes=VMEM_LIMIT_BYTES)


def _dot(a, b):
    return jnp.dot(a.astype(BF16), b.astype(BF16), preferred_element_type=F32)


def _dot_nt(a, b):
    return lax.dot_general(a.astype(BF16), b.astype(BF16), (((1,), (1,)), ((), ())),
                           preferred_element_type=F32)


def _dot_tn(a, b):
    return lax.dot_general(a.astype(BF16), b.astype(BF16), (((0,), (0,)), ((), ())),
                           preferred_element_type=F32)


def _sigmoid(x):
    return 1.0 / (1.0 + jnp.exp(-x))


def _silu(x):
    return x * _sigmoid(x)


def _softplus(x):
    return jnp.maximum(x, 0.0) + jnp.log(1.0 + jnp.exp(-jnp.abs(x)))


def _layer_norm(y, g, b):
    mu = jnp.mean(y, axis=-1, keepdims=True)
    yc = y - mu
    var = jnp.mean(yc * yc, axis=-1, keepdims=True)
    return yc * lax.rsqrt(var + LN_EPS) * g + b


def _pick_tile(n, candidates):
    for c in candidates:
        if n % c == 0:
            return c
    raise ValueError(f"no tile for {n} in {candidates}")


def _mm_kernel(x_ref, w_ref, o_ref):
    o_ref[...] = jnp.dot(x_ref[...], w_ref[...], preferred_element_type=F32).astype(o_ref.dtype)


def _matmul(x, w, out_dtype):
    m, k = x.shape
    nl, _, n = w.shape
    tm = _pick_tile(m, (1024, 512))
    tn = _pick_tile(n, (1024, 512))
    return pl.pallas_call(
        _mm_kernel,
        grid=(nl, m // tm, n // tn),
        in_specs=[pl.BlockSpec((tm, k), lambda l, i, j: (i, 0)),
                  pl.BlockSpec((None, k, tn), lambda l, i, j: (l, 0, j))],
        out_specs=pl.BlockSpec((None, tm, tn), lambda l, i, j: (l, i, j)),
        out_shape=jax.ShapeDtypeStruct((nl, m, n), out_dtype),
        compiler_params=_cparams("parallel", "parallel", "arbitrary"),
    )(x, w)


def _mm_gated_kernel(x_ref, w1_ref, w2_ref, o_ref, *, mode):
    x = x_ref[...]
    a = jnp.dot(x, w1_ref[...], preferred_element_type=F32)
    b = jnp.dot(x, w2_ref[...], preferred_element_type=F32)
    r = a * _sigmoid(b) if mode == "glu" else _silu(a) * b
    o_ref[...] = r.astype(o_ref.dtype)


def _mm_gated(x, w1, w2, n, off2, mode, out_dtype):
    m, k = x.shape
    tm = _pick_tile(m, (1024, 512))
    tn = 512
    ob = off2 // tn
    return pl.pallas_call(
        functools.partial(_mm_gated_kernel, mode=mode),
        grid=(m // tm, n // tn),
        in_specs=[pl.BlockSpec((tm, k), lambda i, j: (i, 0)),
                  pl.BlockSpec((None, k, tn), lambda i, j: (0, 0, j)),
                  pl.BlockSpec((None, k, tn), lambda i, j: (0, 0, ob + j))],
        out_specs=pl.BlockSpec((tm, tn), lambda i, j: (i, j)),
        out_shape=jax.ShapeDtypeStruct((m, n), out_dtype),
        compiler_params=_cparams("parallel", "arbitrary"),
    )(x, w1, w2)


def _mm_res_ln_kernel(x_ref, w_ref, res_ref, g_ref, b_ref, o_ref, obf_ref, acc_ref, *, nk):
    kk = pl.program_id(1)

    @pl.when(kk == 0)
    def _():
        acc_ref[...] = jnp.zeros_like(acc_ref)

    acc_ref[...] += jnp.dot(x_ref[...], w_ref[...], preferred_element_type=F32)

    @pl.when(kk == nk - 1)
    def _():
        y = _layer_norm(ALPHA * res_ref[...] + acc_ref[...], g_ref[...], b_ref[...])
        o_ref[...] = y
        obf_ref[...] = y.astype(BF16)


def _mm_res_ln(x, w, res, g, b):
    m, k = x.shape
    d = w.shape[1]
    tm = 512
    tk = _pick_tile(k, (1024, 512))
    nk = k // tk
    return pl.pallas_call(
        functools.partial(_mm_res_ln_kernel, nk=nk),
        grid=(m // tm, nk),
        in_specs=[pl.BlockSpec((tm, tk), lambda i, kk: (i, kk)),
                  pl.BlockSpec((tk, d), lambda i, kk: (kk, 0)),
                  pl.BlockSpec((tm, d), lambda i, kk: (i, 0)),
                  pl.BlockSpec((1, d), lambda i, kk: (0, 0)),
                  pl.BlockSpec((1, d), lambda i, kk: (0, 0))],
        out_specs=[pl.BlockSpec((tm, d), lambda i, kk: (i, 0)),
                   pl.BlockSpec((tm, d), lambda i, kk: (i, 0))],
        out_shape=[jax.ShapeDtypeStruct((m, d), F32), jax.ShapeDtypeStruct((m, d), BF16)],
        scratch_shapes=[pltpu.VMEM((tm, d), F32)],
        compiler_params=_cparams("parallel", "arbitrary"),
    )(x, w, res, g.reshape(1, d), b.reshape(1, d))


def _gla_gate_kernel(x_ref, wgl_ref, wup_ref, b_ref, o_ref):
    gl = jnp.dot(x_ref[...], wgl_ref[...], preferred_element_type=F32)
    z = _dot(gl, wup_ref[...]) + b_ref[...]
    o_ref[...] = (jnp.minimum(z, 0.0) - jnp.log(1.0 + jnp.exp(-jnp.abs(z)))) * (1.0 / GLA_GATE_NORM)


def _gla_gate(x, w_gl, w_up, b_gk):
    m, k = x.shape
    tm = 512
    wgl = jnp.zeros((k, LANES), BF16).at[:, :GLA_GATE_RANK].set(w_gl.astype(BF16))
    wup = jnp.zeros((LANES, GLA_QK), BF16).at[:GLA_GATE_RANK].set(w_up.astype(BF16))
    return pl.pallas_call(
        _gla_gate_kernel,
        grid=(m // tm,),
        in_specs=[pl.BlockSpec((tm, k), lambda i: (i, 0)),
                  pl.BlockSpec((k, LANES), lambda i: (0, 0)),
                  pl.BlockSpec((LANES, GLA_QK), lambda i: (0, 0)),
                  pl.BlockSpec((1, GLA_QK), lambda i: (0, 0))],
        out_specs=pl.BlockSpec((tm, GLA_QK), lambda i: (i, 0)),
        out_shape=jax.ShapeDtypeStruct((m, GLA_QK), F32),
        compiler_params=_cparams("parallel"),
    )(x, wgl, wup, b_gk.reshape(1, GLA_QK).astype(F32))


def _gdn_gate_kernel(x_ref, w_ref, alog_ref, dtb_ref, o_ref):
    p = jnp.dot(x_ref[...], w_ref[...], preferred_element_type=F32)
    lane = lax.broadcasted_iota(jnp.int32, p.shape, 1)
    beta = _sigmoid(p)
    g = -jnp.exp(alog_ref[...]) * _softplus(p + dtb_ref[...])
    o_ref[...] = jnp.where(lane < GDN_V_HEADS, beta, jnp.where(lane < 2 * GDN_V_HEADS, g, 0.0))


def _gdn_gate(x, w_ba, a_log, dt_bias):
    m, k = x.shape
    tm = 512
    hv = GDN_V_HEADS
    w = jnp.zeros((k, LANES), BF16).at[:, :2 * hv].set(w_ba.astype(BF16))
    alog = jnp.zeros((1, LANES), F32).at[0, hv:2 * hv].set(a_log.astype(F32))
    dtb = jnp.zeros((1, LANES), F32).at[0, hv:2 * hv].set(dt_bias.astype(F32))
    return pl.pallas_call(
        _gdn_gate_kernel,
        grid=(m // tm,),
        in_specs=[pl.BlockSpec((tm, k), lambda i: (i, 0)),
                  pl.BlockSpec((k, LANES), lambda i: (0, 0)),
                  pl.BlockSpec((1, LANES), lambda i: (0, 0)),
                  pl.BlockSpec((1, LANES), lambda i: (0, 0))],
        out_specs=pl.BlockSpec((tm, LANES), lambda i: (i, 0)),
        out_shape=jax.ShapeDtypeStruct((m, LANES), F32),
        compiler_params=_cparams("parallel"),
    )(x, w, alog, dtb)


def _router_kernel(x_ref, w_ref, b_ref, o_ref):
    logits = _dot(x_ref[...], w_ref[...]) + b_ref[...]
    lane = lax.broadcasted_iota(jnp.int32, logits.shape, 1)
    neg = -jnp.inf
    logits = jnp.where(lane < N_EXPERTS, logits, neg)
    m1 = jnp.max(logits, axis=1, keepdims=True)
    i1 = jnp.min(jnp.where(logits == m1, lane, LANES), axis=1, keepdims=True)
    rest = jnp.where(lane == i1, neg, logits)
    m2 = jnp.max(rest, axis=1, keepdims=True)
    i2 = jnp.min(jnp.where(rest == m2, lane, LANES), axis=1, keepdims=True)
    e = jnp.exp(m2 - m1)
    denom = 1.0 + e
    sel = jnp.where(lane == 0, i1.astype(F32), jnp.where(lane == 1, i2.astype(F32), 0.0))
    o_ref[...] = jnp.where(lane == 2, 1.0 / denom, jnp.where(lane == 3, e / denom, sel))


def _router(x, router_w, router_b):
    m, k = x.shape
    tm = 512
    w = jnp.zeros((k, LANES), BF16).at[:, :N_EXPERTS].set(router_w.astype(BF16))
    b = jnp.zeros((1, LANES), F32).at[0, :N_EXPERTS].set(router_b.astype(F32))
    return pl.pallas_call(
        _router_kernel,
        grid=(m // tm,),
        in_specs=[pl.BlockSpec((tm, k), lambda i: (i, 0)),
                  pl.BlockSpec((k, LANES), lambda i: (0, 0)),
                  pl.BlockSpec((1, LANES), lambda i: (0, 0))],
        out_specs=pl.BlockSpec((tm, LANES), lambda i: (i, 0)),
        out_shape=jax.ShapeDtypeStruct((m, LANES), F32),
        compiler_params=_cparams("parallel"),
    )(x, w, b)


def _row_copy(src_hbm, dst, sem, src_row, dst_row):
    return pltpu.make_async_copy(src_hbm.at[pl.ds(src_row, 1)], dst.at[pl.ds(dst_row, 1)], sem)


def _gather_rows(src_hbm, idx_ref, dst, sem, n):
    def start(r, carry):
        _row_copy(src_hbm, dst, sem, idx_ref[0, r], r).start()
        return carry

    def wait(r, carry):
        _row_copy(src_hbm, dst, sem, idx_ref[0, r], r).wait()
        return carry

    lax.fori_loop(0, n, start, 0)
    lax.fori_loop(0, n, wait, 0)


def _moe_gather_kernel(idx_ref, x_hbm, o_ref, buf_ref, sem, *, tg):
    _gather_rows(x_hbm, idx_ref, buf_ref, sem, tg)
    o_ref[...] = buf_ref[...].astype(o_ref.dtype)


def _moe_gather(x, src, tg):
    d = x.shape[1]
    n = src.shape[0]
    return pl.pallas_call(
        functools.partial(_moe_gather_kernel, tg=tg),
        grid=(n // tg,),
        in_specs=[pl.BlockSpec((None, 1, tg), lambda i: (i, 0, 0), memory_space=pltpu.SMEM),
                  pl.BlockSpec(memory_space=pl.ANY)],
        out_specs=pl.BlockSpec((tg, d), lambda i: (i, 0)),
        out_shape=jax.ShapeDtypeStruct((n, d), BF16),
        scratch_shapes=[pltpu.VMEM((tg, d), F32), pltpu.SemaphoreType.DMA(())],
        compiler_params=_cparams("arbitrary"),
    )(src.reshape(n // tg, 1, tg), x)


def _moe_up_kernel(te_ref, na_ref, x_ref, wg_ref, wu_ref, o_ref):
    i = pl.program_id(1)

    @pl.when(i < na_ref[0])
    def _():
        x = x_ref[...]
        a = jnp.dot(x, wg_ref[...], preferred_element_type=F32)
        b = jnp.dot(x, wu_ref[...], preferred_element_type=F32)
        o_ref[...] = (_silu(a) * b).astype(o_ref.dtype)

    @pl.when(i >= na_ref[0])
    def _():
        o_ref[...] = jnp.zeros_like(o_ref)


def _moe_up(xg, w_gate, w_up, tile_e, n_active, tm):
    n, k = xg.shape
    f = w_gate.shape[2]
    tn = _pick_tile(f, (1024, 512))
    return pl.pallas_call(
        _moe_up_kernel,
        grid_spec=pltpu.PrefetchScalarGridSpec(
            num_scalar_prefetch=2,
            grid=(f // tn, n // tm),
            in_specs=[pl.BlockSpec((tm, k), lambda j, i, te, na: (i, 0)),
                      pl.BlockSpec((None, k, tn), lambda j, i, te, na: (te[i], 0, j)),
                      pl.BlockSpec((None, k, tn), lambda j, i, te, na: (te[i], 0, j))],
            out_specs=pl.BlockSpec((tm, tn), lambda j, i, te, na: (i, j))),
        out_shape=jax.ShapeDtypeStruct((n, f), BF16),
        compiler_params=_cparams("parallel", "arbitrary"),
    )(tile_e, n_active, xg, w_gate, w_up)


def _moe_down_kernel(te_ref, na_ref, h_ref, w_ref, o_ref):
    i = pl.program_id(1)

    @pl.when(i < na_ref[0])
    def _():
        o_ref[...] = jnp.dot(h_ref[...], w_ref[...], preferred_element_type=F32)

    @pl.when(i >= na_ref[0])
    def _():
        o_ref[...] = jnp.zeros_like(o_ref)


def _moe_down(h, w_down, tile_e, n_active, tm):
    n, f = h.shape
    d = w_down.shape[2]
    tn = 512
    return pl.pallas_call(
        _moe_down_kernel,
        grid_spec=pltpu.PrefetchScalarGridSpec(
            num_scalar_prefetch=2,
            grid=(d // tn, n // tm),
            in_specs=[pl.BlockSpec((tm, f), lambda j, i, te, na: (i, 0)),
                      pl.BlockSpec((None, f, tn), lambda j, i, te, na: (te[i], 0, j))],
            out_specs=pl.BlockSpec((tm, tn), lambda j, i, te, na: (i, j))),
        out_shape=jax.ShapeDtypeStruct((n, d), F32),
        compiler_params=_cparams("parallel", "arbitrary"),
    )(tile_e, n_active, h, w_down)


def _moe_combine_kernel(p1_ref, p2_ref, ys_hbm, sel_ref, res_ref, g_ref, b_ref, o_ref, buf_ref, sem, *, tt):
    _gather_rows(ys_hbm, p1_ref, buf_ref.at[0], sem.at[0], tt)
    _gather_rows(ys_hbm, p2_ref, buf_ref.at[1], sem.at[1], tt)
    sel = sel_ref[...]
    lane = lax.broadcasted_iota(jnp.int32, sel.shape, 1)
    w1 = jnp.sum(jnp.where(lane == 2, sel, 0.0), axis=1, keepdims=True)
    w2 = jnp.sum(jnp.where(lane == 3, sel, 0.0), axis=1, keepdims=True)
    ff = buf_ref[0] * w1 + buf_ref[1] * w2
    o_ref[...] = _layer_norm(ALPHA * res_ref[...] + ff, g_ref[...], b_ref[...])


def _moe_combine(ys, pos1, pos2, sel, res, g, b):
    m, d = res.shape
    tt = 256
    idx_spec = pl.BlockSpec((None, 1, tt), lambda i: (i, 0, 0), memory_space=pltpu.SMEM)
    return pl.pallas_call(
        functools.partial(_moe_combine_kernel, tt=tt),
        grid=(m // tt,),
        in_specs=[idx_spec, idx_spec,
                  pl.BlockSpec(memory_space=pl.ANY),
                  pl.BlockSpec((tt, LANES), lambda i: (i, 0)),
                  pl.BlockSpec((tt, d), lambda i: (i, 0)),
                  pl.BlockSpec((1, d), lambda i: (0, 0)),
                  pl.BlockSpec((1, d), lambda i: (0, 0))],
        out_specs=pl.BlockSpec((tt, d), lambda i: (i, 0)),
        out_shape=jax.ShapeDtypeStruct((m, d), F32),
        scratch_shapes=[pltpu.VMEM((2, tt, d), F32), pltpu.SemaphoreType.DMA((2,))],
        compiler_params=_cparams("arbitrary"),
    )(pos1.reshape(m // tt, 1, tt), pos2.reshape(m // tt, 1, tt), ys, sel, res, g.reshape(1, d), b.reshape(1, d))


def _moe_plan(sel, tm):
    m = sel.shape[0]
    i1 = sel[:, 0].astype(jnp.int32)
    i2 = sel[:, 1].astype(jnp.int32)
    e = jnp.concatenate([i1, i2])
    onehot = (e[:, None] == jnp.arange(N_EXPERTS, dtype=jnp.int32)[None, :]).astype(jnp.int32)
    csum = jnp.cumsum(onehot, axis=0)
    rank = jnp.sum(csum * onehot, axis=1) - 1
    counts = csum[-1]
    padded = ((counts + tm - 1) // tm) * tm
    ends = jnp.cumsum(padded)
    starts = ends - padded
    pos = jnp.sum(starts[None, :] * onehot, axis=1) + rank
    n_slots = 2 * m + N_EXPERTS * tm
    tok = jnp.arange(m, dtype=jnp.int32)
    src = jnp.zeros((n_slots,), jnp.int32).at[pos].set(jnp.concatenate([tok, tok]))
    tile_start = jnp.arange(n_slots // tm, dtype=jnp.int32) * tm
    tile_e = jnp.sum((tile_start[:, None] >= ends[None, :]).astype(jnp.int32), axis=1)
    n_active = (ends[-1] // tm).astype(jnp.int32).reshape(1)
    return src, pos[:m], pos[m:], jnp.minimum(tile_e, N_EXPERTS - 1), n_active


def _moe(x_a, x_b, p, g, b):
    tm = 512
    ma = x_a.shape[0]
    x = jnp.concatenate([x_a, x_b], axis=0)
    sel = _router(x, p["router_w"], p["router_b"])
    src, pos1, pos2, tile_e, n_active = _moe_plan(sel, tm)
    xg = _moe_gather(x, src, tm)
    hdn = _moe_up(xg, p["moe_w_gate"], p["moe_w_up"], tile_e, n_active, tm)
    ys = _moe_down(hdn, p["moe_w_down"], tile_e, n_active, tm)
    return (_moe_combine(ys, pos1[:ma], pos2[:ma], sel[:ma], x_a, g, b),
            _moe_combine(ys, pos1[ma:], pos2[ma:], sel[ma:], x_b, g, b))


def _gla_kernel(q_ref, k_ref, v_ref, la_ref, r_ref, s0_ref, ng_ref, o_ref, sout_ref, st_ref, bc_ref,
                *, tt, nt):
    sub = GLA_SUB
    i = pl.program_id(2)

    @pl.when(i == 0)
    def _():
        st_ref[...] = s0_ref[...].T

    rows = lax.broadcasted_iota(jnp.int32, (tt, tt), 0)
    cols = lax.broadcasted_iota(jnp.int32, (tt, tt), 1)
    tri = jnp.where(((rows & -sub) == (cols & -sub)) & (cols <= rows), 1.0, 0.0).astype(F32)
    bc_ref[...] = jnp.dot(tri, la_ref[...], precision=lax.Precision.HIGHEST, preferred_element_type=F32)
    trow = lax.broadcasted_iota(jnp.int32, (sub, GLA_DK), 0)
    lane = lax.broadcasted_iota(jnp.int32, (sub, LANES), 1)
    ng = ng_ref[...]

    rsl = [slice(c * sub, (c + 1) * sub) for c in range(tt // sub)]
    b = [bc_ref[r, :] for r in rsl]
    q = [q_ref[r, :] * (GLA_DK ** -0.5) for r in rsl]
    k = [k_ref[r, :] for r in rsl]
    att = [jnp.zeros((sub, LANES), F32) for _ in rsl]
    for s in range(sub):
        for c in range(len(rsl)):
            e = jnp.exp(jnp.where(trow >= s, b[c] - b[c][s:s + 1, :], -jnp.inf))
            col = jnp.sum(q[c] * k[c][s:s + 1, :] * e, axis=1, keepdims=True)
            att[c] = jnp.where(lane == s, col, att[c])
    intra = [_dot(att[c][:, :sub], v_ref[r, :]) for c, r in enumerate(rsl)]
    b_last = [bb[sub - 1:sub, :] for bb in b]
    upd = [_dot_tn(v_ref[r, :], k[c] * jnp.exp(b_last[c] - b[c])) for c, r in enumerate(rsl)]

    st = st_ref[...]
    for c, r in enumerate(rsl):
        o = intra[c] + _dot_nt(q[c] * jnp.exp(b[c]), st)
        st = st * jnp.exp(b_last[c]) + upd[c]
        ms = jnp.mean(o * o, axis=1, keepdims=True)
        o_ref[r, :] = (o * lax.rsqrt(ms + LN_EPS) * ng * _silu(r_ref[r, :])).astype(o_ref.dtype)
    st_ref[...] = st

    @pl.when(i == nt - 1)
    def _():
        sout_ref[...] = st_ref[...].T


def _gla(qkvr, log_a, s0, norm_g, bsz, t):
    tt = _pick_tile(t, (128, 16))
    nt = t // tt
    h = GLA_HEADS
    vb = GLA_QK * 2 // GLA_DV
    rb = vb + GLA_V // GLA_DV
    row = lambda b, hh, i: b * nt + i
    return pl.pallas_call(
        functools.partial(_gla_kernel, tt=tt, nt=nt),
        grid=(bsz, h, nt),
        in_specs=[pl.BlockSpec((tt, GLA_DK), lambda b, hh, i: (row(b, hh, i), hh)),
                  pl.BlockSpec((tt, GLA_DK), lambda b, hh, i: (row(b, hh, i), h + hh)),
                  pl.BlockSpec((tt, GLA_DV), lambda b, hh, i: (row(b, hh, i), vb + hh)),
                  pl.BlockSpec((tt, GLA_DK), lambda b, hh, i: (row(b, hh, i), hh)),
                  pl.BlockSpec((tt, GLA_DV), lambda b, hh, i: (row(b, hh, i), rb + hh)),
                  pl.BlockSpec((None, None, GLA_DK, GLA_DV), lambda b, hh, i: (b, hh, 0, 0)),
                  pl.BlockSpec((1, GLA_DV), lambda b, hh, i: (0, 0))],
        out_specs=[pl.BlockSpec((tt, GLA_DV), lambda b, hh, i: (row(b, hh, i), hh)),
                   pl.BlockSpec((None, None, GLA_DK, GLA_DV), lambda b, hh, i: (b, hh, 0, 0))],
        out_shape=[jax.ShapeDtypeStruct((bsz * t, GLA_V), BF16),
                   jax.ShapeDtypeStruct((bsz, h, GLA_DK, GLA_DV), F32)],
        scratch_shapes=[pltpu.VMEM((GLA_DV, GLA_DK), F32), pltpu.VMEM((tt, GLA_DK), F32)],
        compiler_params=_cparams("parallel", "parallel", "arbitrary"),
    )(qkvr, qkvr, qkvr, log_a, qkvr, s0, norm_g.reshape(1, GLA_DV).astype(F32))


def _conf_conv_kernel(u_ref, buf_ref, w_ref, dwb_ref, g_ref, b_ref, c_ref, bufo_ref, ext_ref, *, tt, nt):
    hist = CONV_W - 1
    pad = 32 - hist
    i = pl.program_id(1)

    @pl.when(i == 0)
    def _():
        ext_ref[0:pad, :] = jnp.zeros((pad, CONV_CH), F32)
        ext_ref[pad:32, :] = buf_ref[...]

    ext_ref[32:32 + tt, :] = u_ref[...]
    cw = 256
    for cb in range(CONV_CH // cw):
        cs = slice(cb * cw, (cb + 1) * cw)
        acc = jnp.zeros((tt, cw), F32)
        for j in range(CONV_W):
            acc = acc + w_ref[j:j + 1, cs] * ext_ref[pad + j:pad + j + tt, cs]
        c_ref[:, cs] = (acc + dwb_ref[:, cs]).astype(c_ref.dtype)
    y = _layer_norm(c_ref[...].astype(F32), g_ref[...], b_ref[...])
    c_ref[...] = _silu(y).astype(c_ref.dtype)

    @pl.when(i == nt - 1)
    def _():
        bufo_ref[...] = ext_ref[tt + pad:tt + 32, :]

    ext_ref[0:32, :] = ext_ref[tt:tt + 32, :]


def _conf_conv(u, buf, dw_w, dw_b, ln_g, ln_b, bsz, t):
    tt = _pick_tile(t, (128, 16))
    nt = t // tt
    vec = lambda a: a.reshape(1, CONV_CH).astype(F32)
    return pl.pallas_call(
        functools.partial(_conf_conv_kernel, tt=tt, nt=nt),
        grid=(bsz, nt),
        in_specs=[pl.BlockSpec((tt, CONV_CH), lambda b, i: (b * nt + i, 0)),
                  pl.BlockSpec((None, CONV_W - 1, CONV_CH), lambda b, i: (b, 0, 0)),
                  pl.BlockSpec((CONV_W, CONV_CH), lambda b, i: (0, 0)),
                  pl.BlockSpec((1, CONV_CH), lambda b, i: (0, 0)),
                  pl.BlockSpec((1, CONV_CH), lambda b, i: (0, 0)),
                  pl.BlockSpec((1, CONV_CH), lambda b, i: (0, 0))],
        out_specs=[pl.BlockSpec((tt, CONV_CH), lambda b, i: (b * nt + i, 0)),
                   pl.BlockSpec((None, CONV_W - 1, CONV_CH), lambda b, i: (b, 0, 0))],
        out_shape=[jax.ShapeDtypeStruct((bsz * t, CONV_CH), F32),
                   jax.ShapeDtypeStruct((bsz, CONV_W - 1, CONV_CH), F32)],
        scratch_shapes=[pltpu.VMEM((32 + tt, CONV_CH), F32)],
        compiler_params=_cparams("parallel", "arbitrary"),
    )(u, buf, dw_w.astype(F32), vec(dw_b), vec(ln_g), vec(ln_b))


def _gdn_conv_kernel(x_ref, buf_ref, w_ref, o_ref, bufo_ref, ext_ref, *, tt, nt, tc):
    hist = GDN_CONV_W - 1
    pad = 8 - hist
    cb = pl.program_id(1)
    i = pl.program_id(2)

    @pl.when(i == 0)
    def _():
        ext_ref[0:pad, :] = jnp.zeros((pad, tc), F32)
        ext_ref[pad:8, :] = buf_ref[...]

    ext_ref[8:8 + tt, :] = x_ref[...]
    acc = jnp.zeros((tt, tc), F32)
    for j in range(GDN_CONV_W):
        acc = acc + w_ref[j:j + 1, :] * ext_ref[pad + j:pad + j + tt, :]
    y = _silu(acc)
    is_qk = cb < (2 * GDN_Q) // tc
    scale = jnp.where(cb < GDN_Q // tc, GDN_DK ** -0.5, 1.0)
    for hh in range(tc // GDN_DK):
        cs = slice(hh * GDN_DK, (hh + 1) * GDN_DK)
        seg = y[:, cs]
        nrm = seg * lax.rsqrt(jnp.sum(seg * seg, axis=1, keepdims=True) + L2_EPS) * scale
        o_ref[:, cs] = jnp.where(is_qk, nrm, seg)

    @pl.when(i == nt - 1)
    def _():
        bufo_ref[...] = ext_ref[tt + pad:tt + 8, :]

    ext_ref[0:8, :] = ext_ref[tt:tt + 8, :]


def _gdn_conv(proj, buf, conv_w, bsz, t):
    tt = _pick_tile(t, (256, 16))
    nt = t // tt
    tc = 1024
    return pl.pallas_call(
        functools.partial(_gdn_conv_kernel, tt=tt, nt=nt, tc=tc),
        grid=(bsz, GDN_CONV_DIM // tc, nt),
        in_specs=[pl.BlockSpec((tt, tc), lambda b, c, i: (b * nt + i, c)),
                  pl.BlockSpec((None, GDN_CONV_W - 1, tc), lambda b, c, i: (b, 0, c)),
                  pl.BlockSpec((GDN_CONV_W, tc), lambda b, c, i: (0, c))],
        out_specs=[pl.BlockSpec((tt, tc), lambda b, c, i: (b * nt + i, c)),
                   pl.BlockSpec((None, GDN_CONV_W - 1, tc), lambda b, c, i: (b, 0, c))],
        out_shape=[jax.ShapeDtypeStruct((bsz * t, GDN_CONV_DIM), F32),
                   jax.ShapeDtypeStruct((bsz, GDN_CONV_W - 1, GDN_CONV_DIM), F32)],
        scratch_shapes=[pltpu.VMEM((8 + tt, tc), F32)],
        compiler_params=_cparams("parallel", "parallel", "arbitrary"),
    )(proj, buf, conv_w.astype(F32))


def _unit_lower_inverses(ms, size):
    rows = lax.broadcasted_iota(jnp.int32, (size, size), 0)
    cols = lax.broadcasted_iota(jnp.int32, (size, size), 1)
    eye = jnp.where(rows == cols, 1.0, 0.0)
    accs = [eye - m for m in ms]
    powers = [_dot(m, m) for m in ms]
    terms = 2
    while 2 * terms < size:
        boths = [_dot(jnp.concatenate([a, p], axis=0), p) for a, p in zip(accs, powers)]
        accs = [a + bt[:size] for a, bt in zip(accs, boths)]
        powers = [bt[size:] for bt in boths]
        terms *= 2
    if terms < size:
        accs = [a + _dot(a, p) for a, p in zip(accs, powers)]
    return accs


GDN_HQ_PER_STEP = 2


def _gdn_kernel(q_ref, k_ref, v_ref, z_ref, gb_ref, s0_ref, ng_ref, o_ref, sout_ref, s_ref,
                *, chunk, nchunk, nt):
    nhq = GDN_HQ_PER_STEP
    rep = GDN_V_HEADS // GDN_QK_HEADS
    nhv = nhq * rep
    hq0 = pl.program_id(1) * nhq
    i = pl.program_id(2)

    @pl.when(i == 0)
    def _():
        s_ref[...] = s0_ref[...]

    lane = lax.broadcasted_iota(jnp.int32, (chunk, LANES), 1)
    rows = lax.broadcasted_iota(jnp.int32, (chunk, chunk), 0)
    cols = lax.broadcasted_iota(jnp.int32, (chunk, chunk), 1)
    incl = cols <= rows
    strict = cols < rows
    ng = ng_ref[...]
    rsl = [slice(c * chunk, (c + 1) * chunk) for c in range(nchunk)]
    vsl = [slice(h * GDN_DV, (h + 1) * GDN_DV) for h in range(nhv)]
    pairs = [(c, a) for c in range(nchunk) for a in range(nhq)]
    chains = [(c, a, j) for c, a in pairs for j in range(rep)]

    qs = {(c, a): q_ref[rsl[c], a * GDN_DK:(a + 1) * GDN_DK] for c, a in pairs}
    ks = {(c, a): k_ref[rsl[c], a * GDN_DK:(a + 1) * GDN_DK] for c, a in pairs}
    kq = {ca: _dot_nt(jnp.concatenate([ks[ca], qs[ca]], axis=0), ks[ca]) for ca in pairs}
    beta, gcum, rel = {}, {}, {}
    for c, a, j in chains:
        gb = gb_ref[rsl[c], :]
        hv = (hq0 + a) * rep + j
        beta[c, a, j] = jnp.sum(jnp.where(lane == hv, gb, 0.0), axis=1, keepdims=True)
        g = jnp.sum(jnp.where(lane == GDN_V_HEADS + hv, gb, 0.0), axis=1, keepdims=True)
        g_row = jnp.sum(jnp.where(rows == cols, g, 0.0), axis=0, keepdims=True)
        gcum[c, a, j] = jnp.sum(jnp.where(incl, g_row, 0.0), axis=1, keepdims=True)
        gcum_row = jnp.sum(jnp.where(rows <= cols, g, 0.0), axis=0, keepdims=True)
        rel[c, a, j] = jnp.exp(jnp.where(incl, gcum[c, a, j] - gcum_row, -jnp.inf))
    tinvs = _unit_lower_inverses(
        [jnp.where(strict, kq[c, a][:chunk] * beta[c, a, j] * rel[c, a, j], 0.0) for c, a, j in chains], chunk)
    uw = {}
    for (c, a, j), tinv in zip(chains, tinvs):
        eg = jnp.exp(gcum[c, a, j])
        v = v_ref[rsl[c], vsl[a * rep + j]]
        uw[c, a, j] = _dot(tinv, jnp.concatenate([v * beta[c, a, j], ks[c, a] * (beta[c, a, j] * eg)], axis=1))

    heads = [(a, j) for a in range(nhq) for j in range(rep)]
    for c in range(nchunk):
        s = [s_ref[h] for h in range(nhv)]
        ws = []
        for h, (a, j) in enumerate(heads):
            wq = jnp.concatenate([uw[c, a, j][:, GDN_DV:], qs[c, a] * jnp.exp(gcum[c, a, j])], axis=0)
            ws.append(_dot(wq, s[h]))
        v_new = [uw[c, a, j][:, :GDN_DV] - ws[h][:chunk] for h, (a, j) in enumerate(heads)]
        outs = [ws[h][chunk:] + _dot(kq[c, a][chunk:] * rel[c, a, j], v_new[h]) for h, (a, j) in enumerate(heads)]
        for h, (a, j) in enumerate(heads):
            g_last = gcum[c, a, j][chunk - 1:chunk, :]
            kd = ks[c, a] * jnp.exp(g_last - gcum[c, a, j])
            s_ref[h] = s[h] * jnp.exp(g_last) + _dot_tn(kd, v_new[h])
        for h in range(nhv):
            o = outs[h]
            ms = jnp.mean(o * o, axis=1, keepdims=True)
            o_ref[rsl[c], vsl[h]] = (o * lax.rsqrt(ms + LN_EPS) * ng * _silu(z_ref[rsl[c], vsl[h]])).astype(o_ref.dtype)

    @pl.when(i == nt - 1)
    def _():
        sout_ref[...] = s_ref[...]


def _gdn(qkv, proj, gb, s0, norm_g, bsz, t):
    chunk = min(CHUNK, t)
    tt = _pick_tile(t, (4 * chunk, chunk))
    nt = t // tt
    nhq = GDN_HQ_PER_STEP
    nhv = nhq * (GDN_V_HEADS // GDN_QK_HEADS)
    qw = nhq * GDN_DK
    vw = nhv * GDN_DV
    row = lambda b, i: b * nt + i
    return pl.pallas_call(
        functools.partial(_gdn_kernel, chunk=chunk, nchunk=tt // chunk, nt=nt),
        grid=(bsz, GDN_QK_HEADS // nhq, nt),
        in_specs=[pl.BlockSpec((tt, qw), lambda b, h, i: (row(b, i), h)),
                  pl.BlockSpec((tt, qw), lambda b, h, i: (row(b, i), GDN_Q // qw + h)),
                  pl.BlockSpec((tt, vw), lambda b, h, i: (row(b, i), 2 * GDN_Q // vw + h)),
                  pl.BlockSpec((tt, vw), lambda b, h, i: (row(b, i), GDN_CONV_DIM // vw + h)),
                  pl.BlockSpec((tt, LANES), lambda b, h, i: (row(b, i), 0)),
                  pl.BlockSpec((None, nhv, GDN_DK, GDN_DV), lambda b, h, i: (b, h, 0, 0)),
                  pl.BlockSpec((1, GDN_DV), lambda b, h, i: (0, 0))],
        out_specs=[pl.BlockSpec((tt, vw), lambda b, h, i: (row(b, i), h)),
                   pl.BlockSpec((None, nhv, GDN_DK, GDN_DV), lambda b, h, i: (b, h, 0, 0))],
        out_shape=[jax.ShapeDtypeStruct((bsz * t, GDN_V), BF16),
                   jax.ShapeDtypeStruct((bsz, GDN_V_HEADS, GDN_DK, GDN_DV), F32)],
        scratch_shapes=[pltpu.VMEM((nhv, GDN_DK, GDN_DV), F32)],
        compiler_params=_cparams("parallel", "parallel", "arbitrary"),
    )(qkv, qkv, qkv, proj, gb, s0, norm_g.reshape(1, GDN_DV).astype(F32))


def _mem_attn_kernel(q_ref, k_ref, v_ref, o_ref):
    for h in range(MEM_HEADS):
        hs = slice(h * MEM_HD, (h + 1) * MEM_HD)
        s = _dot_nt(q_ref[:, hs], k_ref[:, h, :]) * (MEM_HD ** -0.5)
        s = s - jnp.max(s, axis=1, keepdims=True)
        p = jnp.exp(s)
        p = p / jnp.sum(p, axis=1, keepdims=True)
        o_ref[:, hs] = _dot(p, v_ref[:, h, :]).astype(o_ref.dtype)


def _mem_attn(q, mk, mv, layer, bsz, t):
    tq = _pick_tile(t, (512, 16))
    nt = t // tq
    kv_spec = pl.BlockSpec((None, None, MEM_LEN, MEM_HEADS, MEM_HD), lambda b, i: (layer, b, 0, 0, 0))
    return pl.pallas_call(
        _mem_attn_kernel,
        grid=(bsz, nt),
        in_specs=[pl.BlockSpec((tq, D_MODEL), lambda b, i: (b * nt + i, 0)), kv_spec, kv_spec],
        out_specs=pl.BlockSpec((tq, D_MODEL), lambda b, i: (b * nt + i, 0)),
        out_shape=jax.ShapeDtypeStruct((bsz * t, D_MODEL), BF16),
        compiler_params=_cparams("parallel", "arbitrary"),
    )(q, mk, mv)


def _trunk(x, bsz, t, gla_s, conf_buf, gdn_s, gdn_buf, mem_k, mem_v, p):
    m = bsz * t
    xb = x.astype(BF16)

    qkvr = _matmul(xb, p["w_qkvr"], F32)[0]
    log_a = _gla_gate(xb, p["w_gl"], p["w_gk_up"], p["b_gk"])
    u = _mm_gated(xb, p["w_glu"], p["w_glu"], CONV_CH, CONV_CH, "glu", F32)
    o_a, gla_new = _gla(qkvr, log_a, gla_s, p["gla_norm_g"], bsz, t)
    c, conf_new = _conf_conv(u, conf_buf, p["conv_dw_w"], p["conv_dw_b"], p["conv_ln_g"], p["conv_ln_b"], bsz, t)
    mix = jnp.concatenate([o_a, c.astype(BF16)], axis=1)
    x, xb = _mm_res_ln(mix, p["w_out_ab"], x, p["ln_mix_g"][0], p["ln_mix_b"][0])
    qm = _matmul(xb, p["w_mq"][0:1], BF16)[0]
    att = _mem_attn(qm, mem_k, mem_v, 0, bsz, t)
    x, xb = _mm_res_ln(att, p["w_mo"][0], x, p["ln_mem_g"][0], p["ln_mem_b"][0])
    hdn = _mm_gated(xb, p["ff_w_gate"], p["ff_w_up"], p["ff_w_gate"].shape[2], 0, "swiglu", BF16)
    x, xb = _mm_res_ln(hdn, p["ff_w_down"], x, p["ln_ff_g"][0], p["ln_ff_b"][0])

    proj = _matmul(xb, p["w_qkvz"], F32)[0]
    gb = _gdn_gate(xb, p["w_ba"], p["gdn_a_log"], p["gdn_dt_bias"])
    qkv, gdnc_new = _gdn_conv(proj, gdn_buf, p["gdn_conv_w"], bsz, t)
    o_c, gdn_new = _gdn(qkv, proj, gb, gdn_s, p["gdn_norm_g"], bsz, t)
    x, xb = _mm_res_ln(o_c, p["w_out_c"], x, p["ln_mix_g"][1], p["ln_mix_b"][1])
    qm = _matmul(xb, p["w_mq"][1:2], BF16)[0]
    att = _mem_attn(qm, mem_k, mem_v, 1, bsz, t)
    x, xb = _mm_res_ln(att, p["w_mo"][1], x, p["ln_mem_g"][1], p["ln_mem_b"][1])
    return x, gla_new, conf_new, gdn_new, gdnc_new


def kernel(x_prompt, x_sample, mem_prompt, state_gla, state_conf_conv, state_gdn, state_gdn_conv, cache_mem_k, cache_mem_v, ln_mix_g, ln_mix_b, ln_mem_g, ln_mem_b, ln_ff_g, ln_ff_b, w_mq, w_mk, w_mv, w_mo, w_in_ab, w_gk_up, b_gk, gla_norm_g, conv_dw_w, conv_dw_b, conv_ln_g, conv_ln_b, w_out_ab, ff_w_gate, ff_w_up, ff_w_down, w_in_c, gdn_conv_w, gdn_a_log, gdn_dt_bias, gdn_norm_g, w_out_c, router_w, router_b, moe_w_gate, moe_w_up, moe_w_down):
    bf = lambda a: a.astype(BF16)
    qkvr_n = 2 * GLA_QK + 2 * GLA_V
    glu_0 = qkvr_n + GLA_GATE_RANK
    qkvz_n = GDN_CONV_DIM + GDN_V
    p = dict(
        ln_mix_g=ln_mix_g, ln_mix_b=ln_mix_b, ln_mem_g=ln_mem_g, ln_mem_b=ln_mem_b, ln_ff_g=ln_ff_g, ln_ff_b=ln_ff_b,
        w_mq=bf(w_mq), w_mo=bf(w_mo),
        w_qkvr=bf(w_in_ab[:, :, :qkvr_n]), w_gl=w_in_ab[0, :, qkvr_n:glu_0], w_glu=bf(w_in_ab[:, :, glu_0:]),
        w_gk_up=w_gk_up[0], b_gk=b_gk[0], gla_norm_g=gla_norm_g[0],
        conv_dw_w=conv_dw_w[0], conv_dw_b=conv_dw_b[0], conv_ln_g=conv_ln_g[0], conv_ln_b=conv_ln_b[0],
        w_out_ab=bf(w_out_ab[0]), ff_w_gate=bf(ff_w_gate), ff_w_up=bf(ff_w_up), ff_w_down=bf(ff_w_down[0]),
        w_qkvz=bf(w_in_c[:, :, :qkvz_n]), w_ba=w_in_c[0, :, qkvz_n:], gdn_conv_w=gdn_conv_w[0],
        gdn_a_log=gdn_a_log[0], gdn_dt_bias=gdn_dt_bias[0], gdn_norm_g=gdn_norm_g[0], w_out_c=bf(w_out_c[0]),
        router_w=router_w[0], router_b=router_b[0],
        moe_w_gate=bf(moe_w_gate[0]), moe_w_up=bf(moe_w_up[0]),
        moe_w_down=bf(moe_w_down[0]),
    )
    nb, nt, d = x_prompt.shape
    sb, st, _ = x_sample.shape
    dt = x_prompt.dtype

    mem = bf(mem_prompt.reshape(nb * MEM_LEN, d))
    kv_shape = (DEPTH, nb, MEM_LEN, MEM_HEADS, MEM_HD)
    mem_k_p = _matmul(mem, bf(w_mk), F32).reshape(kv_shape)
    mem_v_p = _matmul(mem, bf(w_mv), F32).reshape(kv_shape)
    y_p, gla_p, conf_p, gdn_p, gdnc_p = _trunk(
        x_prompt.reshape(nb * nt, d), nb, nt,
        jnp.zeros((nb, GLA_HEADS, GLA_DK, GLA_DV), dt), jnp.zeros((nb, CONV_W - 1, CONV_CH), dt),
        jnp.zeros((nb, GDN_V_HEADS, GDN_DK, GDN_DV), dt), jnp.zeros((nb, GDN_CONV_W - 1, GDN_CONV_DIM), dt),
        mem_k_p, mem_v_p, p)

    y_s, gla_s, conf_s, gdn_s, gdnc_s = _trunk(
        x_sample.reshape(sb * st, d), sb, st, state_gla[0], state_conf_conv[0], state_gdn[0], state_gdn_conv[0],
        cache_mem_k, cache_mem_v, p)

    y_p, y_s = _moe(y_p, y_s, p, ln_ff_g[1], ln_ff_b[1])

    return (y_p.reshape(nb, nt, d), y_s.reshape(sb, st, d),
            gla_p[None], conf_p[None], gdn_p[None], gdnc_p[None], mem_k_p, mem_v_p,
            gla_s[None], conf_s[None], gdn_s[None], gdnc_s[None])
```

```python
import functools

import jax
import jax.numpy as jnp
from jax import lax
from jax.experimental import pallas as pl
from jax.experimental.pallas import tpu as pltpu

F32 = jnp.float32
BF16 = jnp.bfloat16

D_MODEL = 2048
DEPTH = 2
ALPHA = (2 * DEPTH) ** 0.25
LN_EPS = 1e-5
L2_EPS = 1e-6
CHUNK = 64

GLA_HEADS = 4
GLA_DK = 128
GLA_DV = 256
GLA_QK = GLA_HEADS * GLA_DK
GLA_V = GLA_HEADS * GLA_DV
GLA_GATE_RANK = 16
GLA_GATE_NORM = 16.0
GLA_SUB = 16
CONV_CH = 1024
CONV_W = 31
GDN_QK_HEADS = 16
GDN_V_HEADS = 32
GDN_DK = 128
GDN_DV = 128
GDN_CONV_W = 4
GDN_Q = GDN_QK_HEADS * GDN_DK
GDN_V = GDN_V_HEADS * GDN_DV
GDN_CONV_DIM = 2 * GDN_Q + GDN_V
MEM_LEN = 256
MEM_HEADS = 4
MEM_HD = D_MODEL // MEM_HEADS
N_EXPERTS = 8
LANES = 128
VMEM_LIMIT_BYTES = 56 * 1024 * 1024


def _cparams(*sem):
    return pltpu.CompilerParams(dimension_semantics=sem, vmem_limit_bytes=VMEM_LIMIT_BYTES)


def _dot(a, b):
    return jnp.dot(a.astype(BF16), b.astype(BF16), preferred_element_type=F32)


def _dot_nt(a, b):
    return lax.dot_general(a.astype(BF16), b.astype(BF16), (((1,), (1,)), ((), ())),
                           preferred_element_type=F32)


def _dot_tn(a, b):
    return lax.dot_general(a.astype(BF16), b.astype(BF16), (((0,), (0,)), ((), ())),
                           preferred_element_type=F32)


def _sigmoid(x):
    return 1.0 / (1.0 + jnp.exp(-x))


def _silu(x):
    return x * _sigmoid(x)


def _softplus(x):
    return jnp.maximum(x, 0.0) + jnp.log(1.0 + jnp.exp(-jnp.abs(x)))


def _layer_norm(y, g, b):
    mu = jnp.mean(y, axis=-1, keepdims=True)
    yc = y - mu
    var = jnp.mean(yc * yc, axis=-1, keepdims=True)
    return yc * lax.rsqrt(var + LN_EPS) * g + b


def _pick_tile(n, candidates):
    for c in candidates:
        if n % c == 0:
            return c
    raise ValueError(f"no tile for {n} in {candidates}")


def _mm_kernel(x_ref, w_ref, o_ref):
    o_ref[...] = jnp.dot(x_ref[...], w_ref[...], preferred_element_type=F32).astype(o_ref.dtype)


def _matmul(x, w, out_dtype):
    m, k = x.shape
    nl, _, n = w.shape
    tm = _pick_tile(m, (1024, 512))
    tn = _pick_tile(n, (1024, 512))
    return pl.pallas_call(
        _mm_kernel,
        grid=(nl, m // tm, n // tn),
        in_specs=[pl.BlockSpec((tm, k), lambda l, i, j: (i, 0)),
                  pl.BlockSpec((None, k, tn), lambda l, i, j: (l, 0, j))],
        out_specs=pl.BlockSpec((None, tm, tn), lambda l, i, j: (l, i, j)),
        out_shape=jax.ShapeDtypeStruct((nl, m, n), out_dtype),
        compiler_params=_cparams("parallel", "parallel", "arbitrary"),
    )(x, w)


def _mm_gated_kernel(x_ref, w1_ref, w2_ref, o_ref, *, mode):
    x = x_ref[...]
    a = jnp.dot(x, w1_ref[...], preferred_element_type=F32)
    b = jnp.dot(x, w2_ref[...], preferred_element_type=F32)
    r = a * _sigmoid(b) if mode == "glu" else _silu(a) * b
    o_ref[...] = r.astype(o_ref.dtype)


def _mm_gated(x, w1, w2, n, off2, mode, out_dtype):
    m, k = x.shape
    tm = _pick_tile(m, (1024, 512))
    tn = 512
    ob = off2 // tn
    return pl.pallas_call(
        functools.partial(_mm_gated_kernel, mode=mode),
        grid=(m // tm, n // tn),
        in_specs=[pl.BlockSpec((tm, k), lambda i, j: (i, 0)),
                  pl.BlockSpec((None, k, tn), lambda i, j: (0, 0, j)),
                  pl.BlockSpec((None, k, tn), lambda i, j: (0, 0, ob + j))],
        out_specs=pl.BlockSpec((tm, tn), lambda i, j: (i, j)),
        out_shape=jax.ShapeDtypeStruct((m, n), out_dtype),
        compiler_params=_cparams("parallel", "arbitrary"),
    )(x, w1, w2)


def _mm_res_ln_kernel(x_ref, w_ref, res_ref, g_ref, b_ref, o_ref, obf_ref, acc_ref, *, nk):
    kk = pl.program_id(1)

    @pl.when(kk == 0)
    def _():
        acc_ref[...] = jnp.zeros_like(acc_ref)

    acc_ref[...] += jnp.dot(x_ref[...], w_ref[...], preferred_element_type=F32)

    @pl.when(kk == nk - 1)
    def _():
        y = _layer_norm(ALPHA * res_ref[...] + acc_ref[...], g_ref[...], b_ref[...])
        o_ref[...] = y
        obf_ref[...] = y.astype(BF16)


def _mm_res_ln(x, w, res, g, b):
    m, k = x.shape
    d = w.shape[1]
    tm = 512
    tk = _pick_tile(k, (2048, 1024, 512))
    nk = k // tk
    return pl.pallas_call(
        functools.partial(_mm_res_ln_kernel, nk=nk),
        grid=(m // tm, nk),
        in_specs=[pl.BlockSpec((tm, tk), lambda i, kk: (i, kk)),
                  pl.BlockSpec((tk, d), lambda i, kk: (kk, 0)),
                  pl.BlockSpec((tm, d), lambda i, kk: (i, 0)),
                  pl.BlockSpec((1, d), lambda i, kk: (0, 0)),
                  pl.BlockSpec((1, d), lambda i, kk: (0, 0))],
        out_specs=[pl.BlockSpec((tm, d), lambda i, kk: (i, 0)),
                   pl.BlockSpec((tm, d), lambda i, kk: (i, 0))],
        out_shape=[jax.ShapeDtypeStruct((m, d), F32), jax.ShapeDtypeStruct((m, d), BF16)],
        scratch_shapes=[pltpu.VMEM((tm, d), F32)],
        compiler_params=_cparams("parallel", "arbitrary"),
    )(x, w, res, g.reshape(1, d), b.reshape(1, d))


def _gla_gate_kernel(x_ref, wgl_ref, wup_ref, b_ref, o_ref):
    gl = jnp.dot(x_ref[...], wgl_ref[...], preferred_element_type=F32)
    z = _dot(gl, wup_ref[...]) + b_ref[...]
    o_ref[...] = (jnp.minimum(z, 0.0) - jnp.log(1.0 + jnp.exp(-jnp.abs(z)))) * (1.0 / GLA_GATE_NORM)


def _gla_gate(x, w_gl, w_up, b_gk):
    m, k = x.shape
    tm = 512
    wgl = jnp.zeros((k, LANES), BF16).at[:, :GLA_GATE_RANK].set(w_gl.astype(BF16))
    wup = jnp.zeros((LANES, GLA_QK), BF16).at[:GLA_GATE_RANK].set(w_up.astype(BF16))
    return pl.pallas_call(
        _gla_gate_kernel,
        grid=(m // tm,),
        in_specs=[pl.BlockSpec((tm, k), lambda i: (i, 0)),
                  pl.BlockSpec((k, LANES), lambda i: (0, 0)),
                  pl.BlockSpec((LANES, GLA_QK), lambda i: (0, 0)),
                  pl.BlockSpec((1, GLA_QK), lambda i: (0, 0))],
        out_specs=pl.BlockSpec((tm, GLA_QK), lambda i: (i, 0)),
        out_shape=jax.ShapeDtypeStruct((m, GLA_QK), F32),
        compiler_params=_cparams("parallel"),
    )(x, wgl, wup, b_gk.reshape(1, GLA_QK).astype(F32))


def _gdn_gate_kernel(x_ref, w_ref, alog_ref, dtb_ref, o_ref):
    p = jnp.dot(x_ref[...], w_ref[...], preferred_element_type=F32)
    lane = lax.broadcasted_iota(jnp.int32, p.shape, 1)
    beta = _sigmoid(p)
    g = -jnp.exp(alog_ref[...]) * _softplus(p + dtb_ref[...])
    o_ref[...] = jnp.where(lane < GDN_V_HEADS, beta, jnp.where(lane < 2 * GDN_V_HEADS, g, 0.0))


def _gdn_gate(x, w_ba, a_log, dt_bias):
    m, k = x.shape
    tm = 512
    hv = GDN_V_HEADS
    w = jnp.zeros((k, LANES), BF16).at[:, :2 * hv].set(w_ba.astype(BF16))
    alog = jnp.zeros((1, LANES), F32).at[0, hv:2 * hv].set(a_log.astype(F32))
    dtb = jnp.zeros((1, LANES), F32).at[0, hv:2 * hv].set(dt_bias.astype(F32))
    return pl.pallas_call(
        _gdn_gate_kernel,
        grid=(m // tm,),
        in_specs=[pl.BlockSpec((tm, k), lambda i: (i, 0)),
                  pl.BlockSpec((k, LANES), lambda i: (0, 0)),
                  pl.BlockSpec((1, LANES), lambda i: (0, 0)),
                  pl.BlockSpec((1, LANES), lambda i: (0, 0))],
        out_specs=pl.BlockSpec((tm, LANES), lambda i: (i, 0)),
        out_shape=jax.ShapeDtypeStruct((m, LANES), F32),
        compiler_params=_cparams("parallel"),
    )(x, w, alog, dtb)


def _router_kernel(x_ref, w_ref, b_ref, o_ref):
    logits = _dot(x_ref[...], w_ref[...]) + b_ref[...]
    lane = lax.broadcasted_iota(jnp.int32, logits.shape, 1)
    neg = -jnp.inf
    logits = jnp.where(lane < N_EXPERTS, logits, neg)
    m1 = jnp.max(logits, axis=1, keepdims=True)
    i1 = jnp.min(jnp.where(logits == m1, lane, LANES), axis=1, keepdims=True)
    rest = jnp.where(lane == i1, neg, logits)
    m2 = jnp.max(rest, axis=1, keepdims=True)
    i2 = jnp.min(jnp.where(rest == m2, lane, LANES), axis=1, keepdims=True)
    e = jnp.exp(m2 - m1)
    denom = 1.0 + e
    sel = jnp.where(lane == 0, i1.astype(F32), jnp.where(lane == 1, i2.astype(F32), 0.0))
    o_ref[...] = jnp.where(lane == 2, 1.0 / denom, jnp.where(lane == 3, e / denom, sel))


def _router(x, router_w, router_b):
    m, k = x.shape
    tm = 512
    w = jnp.zeros((k, LANES), BF16).at[:, :N_EXPERTS].set(router_w.astype(BF16))
    b = jnp.zeros((1, LANES), F32).at[0, :N_EXPERTS].set(router_b.astype(F32))
    return pl.pallas_call(
        _router_kernel,
        grid=(m // tm,),
        in_specs=[pl.BlockSpec((tm, k), lambda i: (i, 0)),
                  pl.BlockSpec((k, LANES), lambda i: (0, 0)),
                  pl.BlockSpec((1, LANES), lambda i: (0, 0))],
        out_specs=pl.BlockSpec((tm, LANES), lambda i: (i, 0)),
        out_shape=jax.ShapeDtypeStruct((m, LANES), F32),
        compiler_params=_cparams("parallel"),
    )(x, w, b)


def _row_copy(src_hbm, dst, sem, src_row, dst_row):
    return pltpu.make_async_copy(src_hbm.at[pl.ds(src_row, 1)], dst.at[pl.ds(dst_row, 1)], sem)


def _gather_rows(src_hbm, idx_ref, dst, sem, n):
    def start(r, carry):
        _row_copy(src_hbm, dst, sem, idx_ref[0, r], r).start()
        return carry

    def wait(r, carry):
        _row_copy(src_hbm, dst, sem, idx_ref[0, r], r).wait()
        return carry

    lax.fori_loop(0, n, start, 0, unroll=8)
    lax.fori_loop(0, n, wait, 0, unroll=8)


def _moe_gather_kernel(idx_ref, x_hbm, o_ref, buf_ref, sem, *, tg):
    _gather_rows(x_hbm, idx_ref, buf_ref, sem, tg)
    o_ref[...] = buf_ref[...].astype(o_ref.dtype)


def _moe_gather(x, src, tg):
    d = x.shape[1]
    n = src.shape[0]
    return pl.pallas_call(
        functools.partial(_moe_gather_kernel, tg=tg),
        grid=(n // tg,),
        in_specs=[pl.BlockSpec((None, 1, tg), lambda i: (i, 0, 0), memory_space=pltpu.SMEM),
                  pl.BlockSpec(memory_space=pl.ANY)],
        out_specs=pl.BlockSpec((tg, d), lambda i: (i, 0)),
        out_shape=jax.ShapeDtypeStruct((n, d), BF16),
        scratch_shapes=[pltpu.VMEM((tg, d), F32), pltpu.SemaphoreType.DMA(())],
        compiler_params=_cparams("arbitrary"),
    )(src.reshape(n // tg, 1, tg), x)


def _moe_up_kernel(te_ref, na_ref, x_ref, wg_ref, wu_ref, o_ref):
    i = pl.program_id(1)

    @pl.when(i < na_ref[0])
    def _():
        x = x_ref[...]
        a = jnp.dot(x, wg_ref[...].astype(BF16), preferred_element_type=F32)
        b = jnp.dot(x, wu_ref[...].astype(BF16), preferred_element_type=F32)
        o_ref[...] = (_silu(a) * b).astype(o_ref.dtype)

    @pl.when(i >= na_ref[0])
    def _():
        o_ref[...] = jnp.zeros_like(o_ref)


def _moe_up(xg, w_gate, w_up, tile_e, n_active, tm):
    n, k = xg.shape
    f = w_gate.shape[2]
    tn = _pick_tile(f, (1024, 512))
    return pl.pallas_call(
        _moe_up_kernel,
        grid_spec=pltpu.PrefetchScalarGridSpec(
            num_scalar_prefetch=2,
            grid=(f // tn, n // tm),
            in_specs=[pl.BlockSpec((tm, k), lambda j, i, te, na: (i, 0)),
                      pl.BlockSpec((None, k, tn), lambda j, i, te, na: (te[i], 0, j)),
                      pl.BlockSpec((None, k, tn), lambda j, i, te, na: (te[i], 0, j))],
            out_specs=pl.BlockSpec((tm, tn), lambda j, i, te, na: (i, j))),
        out_shape=jax.ShapeDtypeStruct((n, f), BF16),
        compiler_params=_cparams("parallel", "arbitrary"),
    )(tile_e, n_active, xg, w_gate, w_up)


def _moe_down_kernel(te_ref, na_ref, h_ref, w_ref, o_ref):
    i = pl.program_id(1)

    @pl.when(i < na_ref[0])
    def _():
        o_ref[...] = jnp.dot(h_ref[...], w_ref[...].astype(BF16), preferred_element_type=F32)

    @pl.when(i >= na_ref[0])
    def _():
        o_ref[...] = jnp.zeros_like(o_ref)


def _moe_down(h, w_down, tile_e, n_active, tm):
    n, f = h.shape
    d = w_down.shape[2]
    tn = 512
    return pl.pallas_call(
        _moe_down_kernel,
        grid_spec=pltpu.PrefetchScalarGridSpec(
            num_scalar_prefetch=2,
            grid=(d // tn, n // tm),
            in_specs=[pl.BlockSpec((tm, f), lambda j, i, te, na: (i, 0)),
                      pl.BlockSpec((None, f, tn), lambda j, i, te, na: (te[i], 0, j))],
            out_specs=pl.BlockSpec((tm, tn), lambda j, i, te, na: (i, j))),
        out_shape=jax.ShapeDtypeStruct((n, d), F32),
        compiler_params=_cparams("parallel", "arbitrary"),
    )(tile_e, n_active, h, w_down)


def _moe_combine_kernel(p1_ref, p2_ref, ys_hbm, sel_ref, res_ref, g_ref, b_ref, o_ref, buf_ref, sem, *, tt):
    _gather_rows(ys_hbm, p1_ref, buf_ref.at[0], sem.at[0], tt)
    _gather_rows(ys_hbm, p2_ref, buf_ref.at[1], sem.at[1], tt)
    sel = sel_ref[...]
    lane = lax.broadcasted_iota(jnp.int32, sel.shape, 1)
    w1 = jnp.sum(jnp.where(lane == 2, sel, 0.0), axis=1, keepdims=True)
    w2 = jnp.sum(jnp.where(lane == 3, sel, 0.0), axis=1, keepdims=True)
    ff = buf_ref[0] * w1 + buf_ref[1] * w2
    o_ref[...] = _layer_norm(ALPHA * res_ref[...] + ff, g_ref[...], b_ref[...])


def _moe_combine(ys, pos1, pos2, sel, res, g, b):
    m, d = res.shape
    tt = 256
    idx_spec = pl.BlockSpec((None, 1, tt), lambda i: (i, 0, 0), memory_space=pltpu.SMEM)
    return pl.pallas_call(
        functools.partial(_moe_combine_kernel, tt=tt),
        grid=(m // tt,),
        in_specs=[idx_spec, idx_spec,
                  pl.BlockSpec(memory_space=pl.ANY),
                  pl.BlockSpec((tt, LANES), lambda i: (i, 0)),
                  pl.BlockSpec((tt, d), lambda i: (i, 0)),
                  pl.BlockSpec((1, d), lambda i: (0, 0)),
                  pl.BlockSpec((1, d), lambda i: (0, 0))],
        out_specs=pl.BlockSpec((tt, d), lambda i: (i, 0)),
        out_shape=jax.ShapeDtypeStruct((m, d), F32),
        scratch_shapes=[pltpu.VMEM((2, tt, d), F32), pltpu.SemaphoreType.DMA((2,))],
        compiler_params=_cparams("arbitrary"),
    )(pos1.reshape(m // tt, 1, tt), pos2.reshape(m // tt, 1, tt), ys, sel, res, g.reshape(1, d), b.reshape(1, d))


def _moe_plan(sel, tm):
    m = sel.shape[0]
    i1 = sel[:, 0].astype(jnp.int32)
    i2 = sel[:, 1].astype(jnp.int32)
    e = jnp.concatenate([i1, i2])
    onehot = (e[:, None] == jnp.arange(N_EXPERTS, dtype=jnp.int32)[None, :]).astype(jnp.int32)
    csum = jnp.cumsum(onehot, axis=0)
    rank = jnp.sum(csum * onehot, axis=1) - 1
    counts = csum[-1]
    padded = ((counts + tm - 1) // tm) * tm
    ends = jnp.cumsum(padded)
    starts = ends - padded
    pos = jnp.sum(starts[None, :] * onehot, axis=1) + rank
    n_slots = 2 * m + N_EXPERTS * tm
    tok = jnp.arange(m, dtype=jnp.int32)
    src = jnp.zeros((n_slots,), jnp.int32).at[pos].set(jnp.concatenate([tok, tok]))
    tile_start = jnp.arange(n_slots // tm, dtype=jnp.int32) * tm
    tile_e = jnp.sum((tile_start[:, None] >= ends[None, :]).astype(jnp.int32), axis=1)
    n_active = (ends[-1] // tm).astype(jnp.int32).reshape(1)
    return src, pos[:m], pos[m:], jnp.minimum(tile_e, N_EXPERTS - 1), n_active


def _moe(x_a, x_b, p, g, b):
    tm = 512
    ma = x_a.shape[0]
    x = jnp.concatenate([x_a, x_b], axis=0)
    sel = _router(x, p["router_w"], p["router_b"])
    src, pos1, pos2, tile_e, n_active = _moe_plan(sel, tm)
    xg = _moe_gather(x, src, tm)
    hdn = _moe_up(xg, p["moe_w_gate"], p["moe_w_up"], tile_e, n_active, tm)
    ys = _moe_down(hdn, p["moe_w_down"], tile_e, n_active, tm)
    return (_moe_combine(ys, pos1[:ma], pos2[:ma], sel[:ma], x_a, g, b),
            _moe_combine(ys, pos1[ma:], pos2[ma:], sel[ma:], x_b, g, b))


def _gla_kernel(q_ref, k_ref, v_ref, la_ref, r_ref, s0_ref, ng_ref, o_ref, sout_ref, st_ref, bc_ref,
                *, tt, nt):
    sub = GLA_SUB
    i = pl.program_id(2)

    @pl.when(i == 0)
    def _():
        st_ref[...] = s0_ref[...].T

    rows = lax.broadcasted_iota(jnp.int32, (tt, tt), 0)
    cols = lax.broadcasted_iota(jnp.int32, (tt, tt), 1)
    tri = jnp.where(((rows & -sub) == (cols & -sub)) & (cols <= rows), 1.0, 0.0).astype(F32)
    bc_ref[...] = jnp.dot(tri, la_ref[...], precision=lax.Precision.HIGHEST, preferred_element_type=F32)
    trow = lax.broadcasted_iota(jnp.int32, (sub, GLA_DK), 0)
    lane = lax.broadcasted_iota(jnp.int32, (sub, LANES), 1)
    ng = ng_ref[...]

    rsl = [slice(c * sub, (c + 1) * sub) for c in range(tt // sub)]
    b = [bc_ref[r, :] for r in rsl]
    q = [q_ref[r, :] * (GLA_DK ** -0.5) for r in rsl]
    k = [k_ref[r, :] for r in rsl]
    att = [jnp.zeros((sub, LANES), F32) for _ in rsl]
    for s in range(sub):
        for c in range(len(rsl)):
            e = jnp.exp(jnp.where(trow >= s, b[c] - b[c][s:s + 1, :], -jnp.inf))
            col = jnp.sum(q[c] * k[c][s:s + 1, :] * e, axis=1, keepdims=True)
            att[c] = jnp.where(lane == s, col, att[c])
    intra = [_dot(att[c][:, :sub], v_ref[r, :]) for c, r in enumerate(rsl)]
    b_last = [bb[sub - 1:sub, :] for bb in b]
    upd = [_dot_tn(v_ref[r, :], k[c] * jnp.exp(b_last[c] - b[c])) for c, r in enumerate(rsl)]

    st = st_ref[...]
    for c, r in enumerate(rsl):
        o = intra[c] + _dot_nt(q[c] * jnp.exp(b[c]), st)
        st = st * jnp.exp(b_last[c]) + upd[c]
        ms = jnp.mean(o * o, axis=1, keepdims=True)
        o_ref[r, :] = (o * lax.rsqrt(ms + LN_EPS) * ng * _silu(r_ref[r, :])).astype(o_ref.dtype)
    st_ref[...] = st

    @pl.when(i == nt - 1)
    def _():
        sout_ref[...] = st_ref[...].T


def _gla(qkvr, log_a, s0, norm_g, bsz, t):
    tt = _pick_tile(t, (128, 16))
    nt = t // tt
    h = GLA_HEADS
    vb = GLA_QK * 2 // GLA_DV
    rb = vb + GLA_V // GLA_DV
    row = lambda b, hh, i: b * nt + i
    return pl.pallas_call(
        functools.partial(_gla_kernel, tt=tt, nt=nt),
        grid=(bsz, h, nt),
        in_specs=[pl.BlockSpec((tt, GLA_DK), lambda b, hh, i: (row(b, hh, i), hh)),
                  pl.BlockSpec((tt, GLA_DK), lambda b, hh, i: (row(b, hh, i), h + hh)),
                  pl.BlockSpec((tt, GLA_DV), lambda b, hh, i: (row(b, hh, i), vb + hh)),
                  pl.BlockSpec((tt, GLA_DK), lambda b, hh, i: (row(b, hh, i), hh)),
                  pl.BlockSpec((tt, GLA_DV), lambda b, hh, i: (row(b, hh, i), rb + hh)),
                  pl.BlockSpec((None, None, GLA_DK, GLA_DV), lambda b, hh, i: (b, hh, 0, 0)),
                  pl.BlockSpec((1, GLA_DV), lambda b, hh, i: (0, 0))],
        out_specs=[pl.BlockSpec((tt, GLA_DV), lambda b, hh, i: (row(b, hh, i), hh)),
                   pl.BlockSpec((None, None, GLA_DK, GLA_DV), lambda b, hh, i: (b, hh, 0, 0))],
        out_shape=[jax.ShapeDtypeStruct((bsz * t, GLA_V), BF16),
                   jax.ShapeDtypeStruct((bsz, h, GLA_DK, GLA_DV), F32)],
        scratch_shapes=[pltpu.VMEM((GLA_DV, GLA_DK), F32), pltpu.VMEM((tt, GLA_DK), F32)],
        compiler_params=_cparams("parallel", "parallel", "arbitrary"),
    )(qkvr, qkvr, qkvr, log_a, qkvr, s0, norm_g.reshape(1, GLA_DV).astype(F32))


def _conf_conv_kernel(u_ref, buf_ref, w_ref, dwb_ref, g_ref, b_ref, c_ref, bufo_ref, ext_ref, *, tt, nt):
    hist = CONV_W - 1
    pad = 32 - hist
    i = pl.program_id(1)

    @pl.when(i == 0)
    def _():
        ext_ref[0:pad, :] = jnp.zeros((pad, CONV_CH), F32)
        ext_ref[pad:32, :] = buf_ref[...]

    ext_ref[32:32 + tt, :] = u_ref[...]
    cw = 256
    for cb in range(CONV_CH // cw):
        cs = slice(cb * cw, (cb + 1) * cw)
        acc = jnp.zeros((tt, cw), F32)
        for j in range(CONV_W):
            acc = acc + w_ref[j:j + 1, cs] * ext_ref[pad + j:pad + j + tt, cs]
        c_ref[:, cs] = (acc + dwb_ref[:, cs]).astype(c_ref.dtype)
    y = _layer_norm(c_ref[...].astype(F32), g_ref[...], b_ref[...])
    c_ref[...] = _silu(y).astype(c_ref.dtype)

    @pl.when(i == nt - 1)
    def _():
        bufo_ref[...] = ext_ref[tt + pad:tt + 32, :]

    ext_ref[0:32, :] = ext_ref[tt:tt + 32, :]


def _conf_conv(u, buf, dw_w, dw_b, ln_g, ln_b, bsz, t):
    tt = _pick_tile(t, (128, 16))
    nt = t // tt
    vec = lambda a: a.reshape(1, CONV_CH).astype(F32)
    return pl.pallas_call(
        functools.partial(_conf_conv_kernel, tt=tt, nt=nt),
        grid=(bsz, nt),
        in_specs=[pl.BlockSpec((tt, CONV_CH), lambda b, i: (b * nt + i, 0)),
                  pl.BlockSpec((None, CONV_W - 1, CONV_CH), lambda b, i: (b, 0, 0)),
                  pl.BlockSpec((CONV_W, CONV_CH), lambda b, i: (0, 0)),
                  pl.BlockSpec((1, CONV_CH), lambda b, i: (0, 0)),
                  pl.BlockSpec((1, CONV_CH), lambda b, i: (0, 0)),
                  pl.BlockSpec((1, CONV_CH), lambda b, i: (0, 0))],
        out_specs=[pl.BlockSpec((tt, CONV_CH), lambda b, i: (b * nt + i, 0)),
                   pl.BlockSpec((None, CONV_W - 1, CONV_CH), lambda b, i: (b, 0, 0))],
        out_shape=[jax.ShapeDtypeStruct((bsz * t, CONV_CH), F32),
                   jax.ShapeDtypeStruct((bsz, CONV_W - 1, CONV_CH), F32)],
        scratch_shapes=[pltpu.VMEM((32 + tt, CONV_CH), F32)],
        compiler_params=_cparams("parallel", "arbitrary"),
    )(u, buf, dw_w.astype(F32), vec(dw_b), vec(ln_g), vec(ln_b))


def _gdn_conv_kernel(x_ref, buf_ref, w_ref, o_ref, bufo_ref, ext_ref, *, tt, nt, tc):
    hist = GDN_CONV_W - 1
    pad = 8 - hist
    cb = pl.program_id(1)
    i = pl.program_id(2)

    @pl.when(i == 0)
    def _():
        ext_ref[0:pad, :] = jnp.zeros((pad, tc), F32)
        ext_ref[pad:8, :] = buf_ref[...]

    ext_ref[8:8 + tt, :] = x_ref[...]
    acc = jnp.zeros((tt, tc), F32)
    for j in range(GDN_CONV_W):
        acc = acc + w_ref[j:j + 1, :] * ext_ref[pad + j:pad + j + tt, :]
    y = _silu(acc)
    is_qk = cb < (2 * GDN_Q) // tc
    scale = jnp.where(cb < GDN_Q // tc, GDN_DK ** -0.5, 1.0)
    for hh in range(tc // GDN_DK):
        cs = slice(hh * GDN_DK, (hh + 1) * GDN_DK)
        seg = y[:, cs]
        nrm = seg * lax.rsqrt(jnp.sum(seg * seg, axis=1, keepdims=True) + L2_EPS) * scale
        o_ref[:, cs] = jnp.where(is_qk, nrm, seg)

    @pl.when(i == nt - 1)
    def _():
        bufo_ref[...] = ext_ref[tt + pad:tt + 8, :]

    ext_ref[0:8, :] = ext_ref[tt:tt + 8, :]


def _gdn_conv(proj, buf, conv_w, bsz, t):
    tt = _pick_tile(t, (256, 16))
    nt = t // tt
    tc = 1024
    return pl.pallas_call(
        functools.partial(_gdn_conv_kernel, tt=tt, nt=nt, tc=tc),
        grid=(bsz, GDN_CONV_DIM // tc, nt),
        in_specs=[pl.BlockSpec((tt, tc), lambda b, c, i: (b * nt + i, c)),
                  pl.BlockSpec((None, GDN_CONV_W - 1, tc), lambda b, c, i: (b, 0, c)),
                  pl.BlockSpec((GDN_CONV_W, tc), lambda b, c, i: (0, c))],
        out_specs=[pl.BlockSpec((tt, tc), lambda b, c, i: (b * nt + i, c)),
                   pl.BlockSpec((None, GDN_CONV_W - 1, tc), lambda b, c, i: (b, 0, c))],
        out_shape=[jax.ShapeDtypeStruct((bsz * t, GDN_CONV_DIM), F32),
                   jax.ShapeDtypeStruct((bsz, GDN_CONV_W - 1, GDN_CONV_DIM), F32)],
        scratch_shapes=[pltpu.VMEM((8 + tt, tc), F32)],
        compiler_params=_cparams("parallel", "parallel", "arbitrary"),
    )(proj, buf, conv_w.astype(F32))


def _unit_lower_inverses(ms, size):
    rows = lax.broadcasted_iota(jnp.int32, (size, size), 0)
    cols = lax.broadcasted_iota(jnp.int32, (size, size), 1)
    eye = jnp.where(rows == cols, 1.0, 0.0)
    accs = [eye - m for m in ms]
    powers = [_dot(m, m) for m in ms]
    terms = 2
    while 2 * terms < size:
        boths = [_dot(jnp.concatenate([a, p], axis=0), p) for a, p in zip(accs, powers)]
        accs = [a + bt[:size] for a, bt in zip(accs, boths)]
        powers = [bt[size:] for bt in boths]
        terms *= 2
    if terms < size:
        accs = [a + _dot(a, p) for a, p in zip(accs, powers)]
    return accs


GDN_HQ_PER_STEP = 2


def _gdn_kernel(q_ref, k_ref, v_ref, z_ref, gb_ref, s0_ref, ng_ref, o_ref, sout_ref, s_ref,
                *, chunk, nchunk, nt):
    nhq = GDN_HQ_PER_STEP
    rep = GDN_V_HEADS // GDN_QK_HEADS
    nhv = nhq * rep
    hq0 = pl.program_id(1) * nhq
    i = pl.program_id(2)

    @pl.when(i == 0)
    def _():
        s_ref[...] = s0_ref[...]

    lane = lax.broadcasted_iota(jnp.int32, (chunk, LANES), 1)
    rows = lax.broadcasted_iota(jnp.int32, (chunk, chunk), 0)
    cols = lax.broadcasted_iota(jnp.int32, (chunk, chunk), 1)
    incl = cols <= rows
    strict = cols < rows
    ng = ng_ref[...]
    rsl = [slice(c * chunk, (c + 1) * chunk) for c in range(nchunk)]
    vsl = [slice(h * GDN_DV, (h + 1) * GDN_DV) for h in range(nhv)]
    pairs = [(c, a) for c in range(nchunk) for a in range(nhq)]
    chains = [(c, a, j) for c, a in pairs for j in range(rep)]

    qs = {(c, a): q_ref[rsl[c], a * GDN_DK:(a + 1) * GDN_DK] for c, a in pairs}
    ks = {(c, a): k_ref[rsl[c], a * GDN_DK:(a + 1) * GDN_DK] for c, a in pairs}
    kq = {ca: _dot_nt(jnp.concatenate([ks[ca], qs[ca]], axis=0), ks[ca]) for ca in pairs}
    beta, gcum, rel = {}, {}, {}
    for c, a, j in chains:
        gb = gb_ref[rsl[c], :]
        hv = (hq0 + a) * rep + j
        beta[c, a, j] = jnp.sum(jnp.where(lane == hv, gb, 0.0), axis=1, keepdims=True)
        g = jnp.sum(jnp.where(lane == GDN_V_HEADS + hv, gb, 0.0), axis=1, keepdims=True)
        g_row = jnp.sum(jnp.where(rows == cols, g, 0.0), axis=0, keepdims=True)
        gcum[c, a, j] = jnp.sum(jnp.where(incl, g_row, 0.0), axis=1, keepdims=True)
        gcum_row = jnp.sum(jnp.where(rows <= cols, g, 0.0), axis=0, keepdims=True)
        rel[c, a, j] = jnp.exp(jnp.where(incl, gcum[c, a, j] - gcum_row, -jnp.inf))
    tinvs = _unit_lower_inverses(
        [jnp.where(strict, kq[c, a][:chunk] * beta[c, a, j] * rel[c, a, j], 0.0) for c, a, j in chains], chunk)
    uw = {}
    for (c, a, j), tinv in zip(chains, tinvs):
        eg = jnp.exp(gcum[c, a, j])
        v = v_ref[rsl[c], vsl[a * rep + j]]
        uw[c, a, j] = _dot(tinv, jnp.concatenate([v * beta[c, a, j], ks[c, a] * (beta[c, a, j] * eg)], axis=1))

    heads = [(a, j) for a in range(nhq) for j in range(rep)]
    for c in range(nchunk):
        s = [s_ref[h] for h in range(nhv)]
        ws = []
        for h, (a, j) in enumerate(heads):
            wq = jnp.concatenate([uw[c, a, j][:, GDN_DV:], qs[c, a] * jnp.exp(gcum[c, a, j])], axis=0)
            ws.append(_dot(wq, s[h]))
        v_new = [uw[c, a, j][:, :GDN_DV] - ws[h][:chunk] for h, (a, j) in enumerate(heads)]
        outs = [ws[h][chunk:] + _dot(kq[c, a][chunk:] * rel[c, a, j], v_new[h]) for h, (a, j) in enumerate(heads)]
        for h, (a, j) in enumerate(heads):
            g_last = gcum[c, a, j][chunk - 1:chunk, :]
            kd = ks[c, a] * jnp.exp(g_last - gcum[c, a, j])
            s_ref[h] = s[h] * jnp.exp(g_last) + _dot_tn(kd, v_new[h])
        for h in range(nhv):
            o = outs[h]
            ms = jnp.mean(o * o, axis=1, keepdims=True)
            o_ref[rsl[c], vsl[h]] = (o * lax.rsqrt(ms + LN_EPS) * ng * _silu(z_ref[rsl[c], vsl[h]])).astype(o_ref.dtype)

    @pl.when(i == nt - 1)
    def _():
        sout_ref[...] = s_ref[...]


def _gdn(qkv, proj, gb, s0, norm_g, bsz, t):
    chunk = min(CHUNK, t)
    tt = _pick_tile(t, (4 * chunk, chunk))
    nt = t // tt
    nhq = GDN_HQ_PER_STEP
    nhv = nhq * (GDN_V_HEADS // GDN_QK_HEADS)
    qw = nhq * GDN_DK
    vw = nhv * GDN_DV
    row = lambda b, i: b * nt + i
    return pl.pallas_call(
        functools.partial(_gdn_kernel, chunk=chunk, nchunk=tt // chunk, nt=nt),
        grid=(bsz, GDN_QK_HEADS // nhq, nt),
        in_specs=[pl.BlockSpec((tt, qw), lambda b, h, i: (row(b, i), h)),
                  pl.BlockSpec((tt, qw), lambda b, h, i: (row(b, i), GDN_Q // qw + h)),
                  pl.BlockSpec((tt, vw), lambda b, h, i: (row(b, i), 2 * GDN_Q // vw + h)),
                  pl.BlockSpec((tt, vw), lambda b, h, i: (row(b, i), GDN_CONV_DIM // vw + h)),
                  pl.BlockSpec((tt, LANES), lambda b, h, i: (row(b, i), 0)),
                  pl.BlockSpec((None, nhv, GDN_DK, GDN_DV), lambda b, h, i: (b, h, 0, 0)),
                  pl.BlockSpec((1, GDN_DV), lambda b, h, i: (0, 0))],
        out_specs=[pl.BlockSpec((tt, vw), lambda b, h, i: (row(b, i), h)),
                   pl.BlockSpec((None, nhv, GDN_DK, GDN_DV), lambda b, h, i: (b, h, 0, 0))],
        out_shape=[jax.ShapeDtypeStruct((bsz * t, GDN_V), BF16),
                   jax.ShapeDtypeStruct((bsz, GDN_V_HEADS, GDN_DK, GDN_DV), F32)],
        scratch_shapes=[pltpu.VMEM((nhv, GDN_DK, GDN_DV), F32)],
        compiler_params=_cparams("parallel", "parallel", "arbitrary"),
    )(qkv, qkv, qkv, proj, gb, s0, norm_g.reshape(1, GDN_DV).astype(F32))


def _mem_attn_kernel(q_ref, k_ref, v_ref, o_ref, kh_ref, vh_ref):
    @pl.when(pl.program_id(1) == 0)
    def _():
        for h in range(MEM_HEADS):
            kh_ref[h] = k_ref[:, h, :].astype(BF16)
            vh_ref[h] = v_ref[:, h, :].astype(BF16)

    for h in range(MEM_HEADS):
        hs = slice(h * MEM_HD, (h + 1) * MEM_HD)
        s = _dot_nt(q_ref[:, hs], kh_ref[h]) * (MEM_HD ** -0.5)
        s = s - jnp.max(s, axis=1, keepdims=True)
        p = jnp.exp(s)
        p = p / jnp.sum(p, axis=1, keepdims=True)
        o_ref[:, hs] = _dot(p, vh_ref[h]).astype(o_ref.dtype)


def _mem_attn(q, mk, mv, layer, bsz, t):
    tq = _pick_tile(t, (512, 16))
    nt = t // tq
    kv_spec = pl.BlockSpec((None, None, MEM_LEN, MEM_HEADS, MEM_HD), lambda b, i: (layer, b, 0, 0, 0))
    return pl.pallas_call(
        _mem_attn_kernel,
        grid=(bsz, nt),
        in_specs=[pl.BlockSpec((tq, D_MODEL), lambda b, i: (b * nt + i, 0)), kv_spec, kv_spec],
        out_specs=pl.BlockSpec((tq, D_MODEL), lambda b, i: (b * nt + i, 0)),
        out_shape=jax.ShapeDtypeStruct((bsz * t, D_MODEL), BF16),
        scratch_shapes=[pltpu.VMEM((MEM_HEADS, MEM_LEN, MEM_HD), BF16),
                        pltpu.VMEM((MEM_HEADS, MEM_LEN, MEM_HD), BF16)],
        compiler_params=_cparams("parallel", "arbitrary"),
    )(q, mk, mv)


def _trunk(x, bsz, t, gla_s, conf_buf, gdn_s, gdn_buf, mem_k, mem_v, p):
    m = bsz * t
    xb = x.astype(BF16)

    qkvr = _matmul(xb, p["w_qkvr"], F32)[0]
    log_a = _gla_gate(xb, p["w_gl"], p["w_gk_up"], p["b_gk"])
    u = _mm_gated(xb, p["w_glu"], p["w_glu"], CONV_CH, CONV_CH, "glu", F32)
    o_a, gla_new = _gla(qkvr, log_a, gla_s, p["gla_norm_g"], bsz, t)
    c, conf_new = _conf_conv(u, conf_buf, p["conv_dw_w"], p["conv_dw_b"], p["conv_ln_g"], p["conv_ln_b"], bsz, t)
    mix = jnp.concatenate([o_a, c.astype(BF16)], axis=1)
    x, xb = _mm_res_ln(mix, p["w_out_ab"], x, p["ln_mix_g"][0], p["ln_mix_b"][0])
    qm = _matmul(xb, p["w_mq"][0:1], BF16)[0]
    att = _mem_attn(qm, mem_k, mem_v, 0, bsz, t)
    x, xb = _mm_res_ln(att, p["w_mo"][0], x, p["ln_mem_g"][0], p["ln_mem_b"][0])
    hdn = _mm_gated(xb, p["ff_w_gate"], p["ff_w_up"], p["ff_w_gate"].shape[2], 0, "swiglu", BF16)
    x, xb = _mm_res_ln(hdn, p["ff_w_down"], x, p["ln_ff_g"][0], p["ln_ff_b"][0])

    proj = _matmul(xb, p["w_qkvz"], F32)[0]
    gb = _gdn_gate(xb, p["w_ba"], p["gdn_a_log"], p["gdn_dt_bias"])
    qkv, gdnc_new = _gdn_conv(proj, gdn_buf, p["gdn_conv_w"], bsz, t)
    o_c, gdn_new = _gdn(qkv, proj, gb, gdn_s, p["gdn_norm_g"], bsz, t)
    x, xb = _mm_res_ln(o_c, p["w_out_c"], x, p["ln_mix_g"][1], p["ln_mix_b"][1])
    qm = _matmul(xb, p["w_mq"][1:2], BF16)[0]
    att = _mem_attn(qm, mem_k, mem_v, 1, bsz, t)
    x, xb = _mm_res_ln(att, p["w_mo"][1], x, p["ln_mem_g"][1], p["ln_mem_b"][1])
    return x, gla_new, conf_new, gdn_new, gdnc_new


def kernel(x_prompt, x_sample, mem_prompt, state_gla, state_conf_conv, state_gdn, state_gdn_conv, cache_mem_k, cache_mem_v, ln_mix_g, ln_mix_b, ln_mem_g, ln_mem_b, ln_ff_g, ln_ff_b, w_mq, w_mk, w_mv, w_mo, w_in_ab, w_gk_up, b_gk, gla_norm_g, conv_dw_w, conv_dw_b, conv_ln_g, conv_ln_b, w_out_ab, ff_w_gate, ff_w_up, ff_w_down, w_in_c, gdn_conv_w, gdn_a_log, gdn_dt_bias, gdn_norm_g, w_out_c, router_w, router_b, moe_w_gate, moe_w_up, moe_w_down):
    bf = lambda a: a.astype(BF16)
    qkvr_n = 2 * GLA_QK + 2 * GLA_V
    glu_0 = qkvr_n + GLA_GATE_RANK
    qkvz_n = GDN_CONV_DIM + GDN_V
    p = dict(
        ln_mix_g=ln_mix_g, ln_mix_b=ln_mix_b, ln_mem_g=ln_mem_g, ln_mem_b=ln_mem_b, ln_ff_g=ln_ff_g, ln_ff_b=ln_ff_b,
        w_mq=bf(w_mq), w_mo=bf(w_mo),
        w_qkvr=bf(w_in_ab[:, :, :qkvr_n]), w_gl=w_in_ab[0, :, qkvr_n:glu_0], w_glu=bf(w_in_ab[:, :, glu_0:]),
        w_gk_up=w_gk_up[0], b_gk=b_gk[0], gla_norm_g=gla_norm_g[0],
        conv_dw_w=conv_dw_w[0], conv_dw_b=conv_dw_b[0], conv_ln_g=conv_ln_g[0], conv_ln_b=conv_ln_b[0],
        w_out_ab=bf(w_out_ab[0]), ff_w_gate=bf(ff_w_gate), ff_w_up=bf(ff_w_up), ff_w_down=bf(ff_w_down[0]),
        w_qkvz=bf(w_in_c[:, :, :qkvz_n]), w_ba=w_in_c[0, :, qkvz_n:], gdn_conv_w=gdn_conv_w[0],
        gdn_a_log=gdn_a_log[0], gdn_dt_bias=gdn_dt_bias[0], gdn_norm_g=gdn_norm_g[0], w_out_c=bf(w_out_c[0]),
        router_w=router_w[0], router_b=router_b[0],
        moe_w_gate=moe_w_gate[0], moe_w_up=moe_w_up[0], moe_w_down=moe_w_down[0],
    )
    nb, nt, d = x_prompt.shape
    sb, st, _ = x_sample.shape
    dt = x_prompt.dtype

    mem = bf(mem_prompt.reshape(nb * MEM_LEN, d))
    kv_shape = (DEPTH, nb, MEM_LEN, MEM_HEADS, MEM_HD)
    mem_k_p = _matmul(mem, bf(w_mk), F32).reshape(kv_shape)
    mem_v_p = _matmul(mem, bf(w_mv), F32).reshape(kv_shape)
    y_p, gla_p, conf_p, gdn_p, gdnc_p = _trunk(
        x_prompt.reshape(nb * nt, d), nb, nt,
        jnp.zeros((nb, GLA_HEADS, GLA_DK, GLA_DV), dt), jnp.zeros((nb, CONV_W - 1, CONV_CH), dt),
        jnp.zeros((nb, GDN_V_HEADS, GDN_DK, GDN_DV), dt), jnp.zeros((nb, GDN_CONV_W - 1, GDN_CONV_DIM), dt),
        mem_k_p, mem_v_p, p)

    y_s, gla_s, conf_s, gdn_s, gdnc_s = _trunk(
        x_sample.reshape(sb * st, d), sb, st, state_gla[0], state_conf_conv[0], state_gdn[0], state_gdn_conv[0],
        cache_mem_k, cache_mem_v, p)

    y_p, y_s = _moe(y_p, y_s, p, ln_ff_g[1], ln_ff_b[1])

    return (y_p.reshape(nb, nt, d), y_s.reshape(sb, st, d),
            gla_p[None], conf_p[None], gdn_p[None], gdnc_p[None], mem_k_p, mem_v_p,
            gla_s[None], conf_s[None], gdn_s[None], gdnc_s[None])
```

```python
import functools

import jax
import jax.numpy as jnp
from jax import lax
from jax.experimental import pallas as pl
from jax.experimental.pallas import tpu as pltpu

F32 = jnp.float32
BF16 = jnp.bfloat16

D_MODEL = 2048
DEPTH = 2
ALPHA = (2 * DEPTH) ** 0.25
LN_EPS = 1e-5
L2_EPS = 1e-6
CHUNK = 64

GLA_HEADS = 4
GLA_DK = 128
GLA_DV = 256
GLA_QK = GLA_HEADS * GLA_DK
GLA_V = GLA_HEADS * GLA_DV
GLA_GATE_RANK = 16
GLA_GATE_NORM = 16.0
GLA_SUB = 16
CONV_CH = 1024
CONV_W = 31
GDN_QK_HEADS = 16
GDN_V_HEADS = 32
GDN_DK = 128
GDN_DV = 128
GDN_CONV_W = 4
GDN_Q = GDN_QK_HEADS * GDN_DK
GDN_V = GDN_V_HEADS * GDN_DV
GDN_CONV_DIM = 2 * GDN_Q + GDN_V
MEM_LEN = 256
MEM_HEADS = 4
MEM_HD = D_MODEL // MEM_HEADS
N_EXPERTS = 8
LANES = 128
VMEM_LIMIT_BYTES = 56 * 1024 * 1024


def _cparams(*sem):
    return pltpu.CompilerParams(dimension_semantics=sem, vmem_limit_bytes=VMEM_LIMIT_BYTES)


def _dot(a, b):
    return jnp.dot(a.astype(BF16), b.astype(BF16), preferred_element_type=F32)


def _dot_nt(a, b):
    return lax.dot_general(a.astype(BF16), b.astype(BF16), (((1,), (1,)), ((), ())),
                           preferred_element_type=F32)


def _dot_tn(a, b):
    return lax.dot_general(a.astype(BF16), b.astype(BF16), (((0,), (0,)), ((), ())),
                           preferred_element_type=F32)


def _sigmoid(x):
    return 1.0 / (1.0 + jnp.exp(-x))


def _silu(x):
    return x * _sigmoid(x)


def _softplus(x):
    return jnp.maximum(x, 0.0) + jnp.log(1.0 + jnp.exp(-jnp.abs(x)))


def _layer_norm(y, g, b):
    mu = jnp.mean(y, axis=-1, keepdims=True)
    yc = y - mu
    var = jnp.mean(yc * yc, axis=-1, keepdims=True)
    return yc * lax.rsqrt(var + LN_EPS) * g + b


def _pick_tile(n, candidates):
    for c in candidates:
        if n % c == 0:
            return c
    raise ValueError(f"no tile for {n} in {candidates}")


def _mm_kernel(x_ref, w_ref, o_ref):
    o_ref[...] = jnp.dot(x_ref[...], w_ref[...], preferred_element_type=F32).astype(o_ref.dtype)


def _matmul(x, w, out_dtype):
    m, k = x.shape
    nl, _, n = w.shape
    tm = _pick_tile(m, (1024, 512))
    tn = _pick_tile(n, (1024, 512))
    return pl.pallas_call(
        _mm_kernel,
        grid=(nl, m // tm, n // tn),
        in_specs=[pl.BlockSpec((tm, k), lambda l, i, j: (i, 0)),
                  pl.BlockSpec((None, k, tn), lambda l, i, j: (l, 0, j))],
        out_specs=pl.BlockSpec((None, tm, tn), lambda l, i, j: (l, i, j)),
        out_shape=jax.ShapeDtypeStruct((nl, m, n), out_dtype),
        compiler_params=_cparams("parallel", "parallel", "arbitrary"),
    )(x, w)


def _mm_gated_kernel(x_ref, w1_ref, w2_ref, o_ref, *, mode):
    x = x_ref[...]
    a = jnp.dot(x, w1_ref[...], preferred_element_type=F32)
    b = jnp.dot(x, w2_ref[...], preferred_element_type=F32)
    r = a * _sigmoid(b) if mode == "glu" else _silu(a) * b
    o_ref[...] = r.astype(o_ref.dtype)


def _mm_gated(x, w1, w2, n, off2, mode, out_dtype):
    m, k = x.shape
    tm = _pick_tile(m, (1024, 512))
    tn = 512
    ob = off2 // tn
    return pl.pallas_call(
        functools.partial(_mm_gated_kernel, mode=mode),
        grid=(m // tm, n // tn),
        in_specs=[pl.BlockSpec((tm, k), lambda i, j: (i, 0)),
                  pl.BlockSpec((None, k, tn), lambda i, j: (0, 0, j)),
                  pl.BlockSpec((None, k, tn), lambda i, j: (0, 0, ob + j))],
        out_specs=pl.BlockSpec((tm, tn), lambda i, j: (i, j)),
        out_shape=jax.ShapeDtypeStruct((m, n), out_dtype),
        compiler_params=_cparams("parallel", "arbitrary"),
    )(x, w1, w2)


def _mm_res_ln_kernel(x_ref, w_ref, res_ref, g_ref, b_ref, o_ref, obf_ref, acc_ref, *, nk):
    kk = pl.program_id(1)

    @pl.when(kk == 0)
    def _():
        acc_ref[...] = jnp.zeros_like(acc_ref)

    acc_ref[...] += jnp.dot(x_ref[...], w_ref[...], preferred_element_type=F32)

    @pl.when(kk == nk - 1)
    def _():
        y = _layer_norm(ALPHA * res_ref[...] + acc_ref[...], g_ref[...], b_ref[...])
        o_ref[...] = y
        obf_ref[...] = y.astype(BF16)


def _mm_res_ln(x, w, res, g, b):
    m, k = x.shape
    d = w.shape[1]
    tm = 512
    tk = _pick_tile(k, (2048, 2816, 1024, 512))
    nk = k // tk
    return pl.pallas_call(
        functools.partial(_mm_res_ln_kernel, nk=nk),
        grid=(m // tm, nk),
        in_specs=[pl.BlockSpec((tm, tk), lambda i, kk: (i, kk)),
                  pl.BlockSpec((tk, d), lambda i, kk: (kk, 0)),
                  pl.BlockSpec((tm, d), lambda i, kk: (i, 0)),
                  pl.BlockSpec((1, d), lambda i, kk: (0, 0)),
                  pl.BlockSpec((1, d), lambda i, kk: (0, 0))],
        out_specs=[pl.BlockSpec((tm, d), lambda i, kk: (i, 0)),
                   pl.BlockSpec((tm, d), lambda i, kk: (i, 0))],
        out_shape=[jax.ShapeDtypeStruct((m, d), F32), jax.ShapeDtypeStruct((m, d), BF16)],
        scratch_shapes=[pltpu.VMEM((tm, d), F32)],
        compiler_params=_cparams("parallel", "arbitrary"),
    )(x, w, res, g.reshape(1, d), b.reshape(1, d))


def _gla_gate_kernel(x_ref, wgl_ref, wup_ref, b_ref, o_ref):
    gl = jnp.dot(x_ref[...], wgl_ref[...], preferred_element_type=F32)
    z = _dot(gl, wup_ref[...]) + b_ref[...]
    o_ref[...] = (jnp.minimum(z, 0.0) - jnp.log(1.0 + jnp.exp(-jnp.abs(z)))) * (1.0 / GLA_GATE_NORM)


def _gla_gate(x, w_gl, w_up, b_gk):
    m, k = x.shape
    tm = 512
    wgl = jnp.zeros((k, LANES), BF16).at[:, :GLA_GATE_RANK].set(w_gl.astype(BF16))
    wup = jnp.zeros((LANES, GLA_QK), BF16).at[:GLA_GATE_RANK].set(w_up.astype(BF16))
    return pl.pallas_call(
        _gla_gate_kernel,
        grid=(m // tm,),
        in_specs=[pl.BlockSpec((tm, k), lambda i: (i, 0)),
                  pl.BlockSpec((k, LANES), lambda i: (0, 0)),
                  pl.BlockSpec((LANES, GLA_QK), lambda i: (0, 0)),
                  pl.BlockSpec((1, GLA_QK), lambda i: (0, 0))],
        out_specs=pl.BlockSpec((tm, GLA_QK), lambda i: (i, 0)),
        out_shape=jax.ShapeDtypeStruct((m, GLA_QK), F32),
        compiler_params=_cparams("parallel"),
    )(x, wgl, wup, b_gk.reshape(1, GLA_QK).astype(F32))


def _gdn_gate_kernel(x_ref, w_ref, alog_ref, dtb_ref, o_ref):
    p = jnp.dot(x_ref[...], w_ref[...], preferred_element_type=F32)
    lane = lax.broadcasted_iota(jnp.int32, p.shape, 1)
    beta = _sigmoid(p)
    g = -jnp.exp(alog_ref[...]) * _softplus(p + dtb_ref[...])
    o_ref[...] = jnp.where(lane < GDN_V_HEADS, beta, jnp.where(lane < 2 * GDN_V_HEADS, g, 0.0))


def _gdn_gate(x, w_ba, a_log, dt_bias):
    m, k = x.shape
    tm = 512
    hv = GDN_V_HEADS
    w = jnp.zeros((k, LANES), BF16).at[:, :2 * hv].set(w_ba.astype(BF16))
    alog = jnp.zeros((1, LANES), F32).at[0, hv:2 * hv].set(a_log.astype(F32))
    dtb = jnp.zeros((1, LANES), F32).at[0, hv:2 * hv].set(dt_bias.astype(F32))
    return pl.pallas_call(
        _gdn_gate_kernel,
        grid=(m // tm,),
        in_specs=[pl.BlockSpec((tm, k), lambda i: (i, 0)),
                  pl.BlockSpec((k, LANES), lambda i: (0, 0)),
                  pl.BlockSpec((1, LANES), lambda i: (0, 0)),
                  pl.BlockSpec((1, LANES), lambda i: (0, 0))],
        out_specs=pl.BlockSpec((tm, LANES), lambda i: (i, 0)),
        out_shape=jax.ShapeDtypeStruct((m, LANES), F32),
        compiler_params=_cparams("parallel"),
    )(x, w, alog, dtb)


def _router_kernel(x_ref, w_ref, b_ref, o_ref):
    logits = _dot(x_ref[...], w_ref[...]) + b_ref[...]
    lane = lax.broadcasted_iota(jnp.int32, logits.shape, 1)
    neg = -jnp.inf
    logits = jnp.where(lane < N_EXPERTS, logits, neg)
    m1 = jnp.max(logits, axis=1, keepdims=True)
    i1 = jnp.min(jnp.where(logits == m1, lane, LANES), axis=1, keepdims=True)
    rest = jnp.where(lane == i1, neg, logits)
    m2 = jnp.max(rest, axis=1, keepdims=True)
    i2 = jnp.min(jnp.where(rest == m2, lane, LANES), axis=1, keepdims=True)
    e = jnp.exp(m2 - m1)
    denom = 1.0 + e
    sel = jnp.where(lane == 0, i1.astype(F32), jnp.where(lane == 1, i2.astype(F32), 0.0))
    o_ref[...] = jnp.where(lane == 2, 1.0 / denom, jnp.where(lane == 3, e / denom, sel))


def _router(x, router_w, router_b):
    m, k = x.shape
    tm = 512
    w = jnp.zeros((k, LANES), BF16).at[:, :N_EXPERTS].set(router_w.astype(BF16))
    b = jnp.zeros((1, LANES), F32).at[0, :N_EXPERTS].set(router_b.astype(F32))
    return pl.pallas_call(
        _router_kernel,
        grid=(m // tm,),
        in_specs=[pl.BlockSpec((tm, k), lambda i: (i, 0)),
                  pl.BlockSpec((k, LANES), lambda i: (0, 0)),
                  pl.BlockSpec((1, LANES), lambda i: (0, 0))],
        out_specs=pl.BlockSpec((tm, LANES), lambda i: (i, 0)),
        out_shape=jax.ShapeDtypeStruct((m, LANES), F32),
        compiler_params=_cparams("parallel"),
    )(x, w, b)


def _row_copy(src_hbm, dst, sem, src_row, dst_row):
    return pltpu.make_async_copy(src_hbm.at[pl.ds(src_row, 1)], dst.at[pl.ds(dst_row, 1)], sem)


def _gather_rows(src_hbm, idx_ref, dst, sem, n):
    def start(r, carry):
        _row_copy(src_hbm, dst, sem, idx_ref[0, r], r).start()
        return carry

    def wait(r, carry):
        _row_copy(src_hbm, dst, sem, idx_ref[0, r], r).wait()
        return carry

    lax.fori_loop(0, n, start, 0, unroll=8)
    lax.fori_loop(0, n, wait, 0, unroll=8)


def _moe_gather_kernel(idx_ref, x_hbm, o_ref, buf_ref, sem, *, tg):
    _gather_rows(x_hbm, idx_ref, buf_ref, sem, tg)
    o_ref[...] = buf_ref[...].astype(o_ref.dtype)


def _moe_gather(x, src, tg):
    d = x.shape[1]
    n = src.shape[0]
    return pl.pallas_call(
        functools.partial(_moe_gather_kernel, tg=tg),
        grid=(n // tg,),
        in_specs=[pl.BlockSpec((None, 1, tg), lambda i: (i, 0, 0), memory_space=pltpu.SMEM),
                  pl.BlockSpec(memory_space=pl.ANY)],
        out_specs=pl.BlockSpec((tg, d), lambda i: (i, 0)),
        out_shape=jax.ShapeDtypeStruct((n, d), BF16),
        scratch_shapes=[pltpu.VMEM((tg, d), F32), pltpu.SemaphoreType.DMA(())],
        compiler_params=_cparams("arbitrary"),
    )(src.reshape(n // tg, 1, tg), x)


def _moe_up_kernel(te_ref, na_ref, x_ref, wg_ref, wu_ref, o_ref):
    i = pl.program_id(1)

    @pl.when(i < na_ref[0])
    def _():
        x = x_ref[...]
        a = jnp.dot(x, wg_ref[...].astype(BF16), preferred_element_type=F32)
        b = jnp.dot(x, wu_ref[...].astype(BF16), preferred_element_type=F32)
        o_ref[...] = (_silu(a) * b).astype(o_ref.dtype)

    @pl.when(i >= na_ref[0])
    def _():
        o_ref[...] = jnp.zeros_like(o_ref)


def _moe_up(xg, w_gate, w_up, tile_e, n_active, tm):
    n, k = xg.shape
    f = w_gate.shape[2]
    tn = _pick_tile(f, (1024, 512))
    return pl.pallas_call(
        _moe_up_kernel,
        grid_spec=pltpu.PrefetchScalarGridSpec(
            num_scalar_prefetch=2,
            grid=(f // tn, n // tm),
            in_specs=[pl.BlockSpec((tm, k), lambda j, i, te, na: (i, 0)),
                      pl.BlockSpec((None, k, tn), lambda j, i, te, na: (te[i], 0, j)),
                      pl.BlockSpec((None, k, tn), lambda j, i, te, na: (te[i], 0, j))],
            out_specs=pl.BlockSpec((tm, tn), lambda j, i, te, na: (i, j))),
        out_shape=jax.ShapeDtypeStruct((n, f), BF16),
        compiler_params=_cparams("parallel", "arbitrary"),
    )(tile_e, n_active, xg, w_gate, w_up)


def _moe_down_kernel(te_ref, na_ref, h_ref, w_ref, o_ref):
    i = pl.program_id(1)

    @pl.when(i < na_ref[0])
    def _():
        o_ref[...] = jnp.dot(h_ref[...], w_ref[...].astype(BF16), preferred_element_type=F32)

    @pl.when(i >= na_ref[0])
    def _():
        o_ref[...] = jnp.zeros_like(o_ref)


def _moe_down(h, w_down, tile_e, n_active, tm):
    n, f = h.shape
    d = w_down.shape[2]
    tn = 512
    return pl.pallas_call(
        _moe_down_kernel,
        grid_spec=pltpu.PrefetchScalarGridSpec(
            num_scalar_prefetch=2,
            grid=(d // tn, n // tm),
            in_specs=[pl.BlockSpec((tm, f), lambda j, i, te, na: (i, 0)),
                      pl.BlockSpec((None, f, tn), lambda j, i, te, na: (te[i], 0, j))],
            out_specs=pl.BlockSpec((tm, tn), lambda j, i, te, na: (i, j))),
        out_shape=jax.ShapeDtypeStruct((n, d), F32),
        compiler_params=_cparams("parallel", "arbitrary"),
    )(tile_e, n_active, h, w_down)


def _moe_combine_kernel(p1_ref, p2_ref, ys_hbm, sel_ref, res_ref, g_ref, b_ref, o_ref, buf_ref, sem, *, tt):
    _gather_rows(ys_hbm, p1_ref, buf_ref.at[0], sem.at[0], tt)
    _gather_rows(ys_hbm, p2_ref, buf_ref.at[1], sem.at[1], tt)
    sel = sel_ref[...]
    lane = lax.broadcasted_iota(jnp.int32, sel.shape, 1)
    w1 = jnp.sum(jnp.where(lane == 2, sel, 0.0), axis=1, keepdims=True)
    w2 = jnp.sum(jnp.where(lane == 3, sel, 0.0), axis=1, keepdims=True)
    ff = buf_ref[0] * w1 + buf_ref[1] * w2
    o_ref[...] = _layer_norm(ALPHA * res_ref[...] + ff, g_ref[...], b_ref[...])


def _moe_combine(ys, pos1, pos2, sel, res, g, b):
    m, d = res.shape
    tt = 512
    idx_spec = pl.BlockSpec((None, 1, tt), lambda i: (i, 0, 0), memory_space=pltpu.SMEM)
    return pl.pallas_call(
        functools.partial(_moe_combine_kernel, tt=tt),
        grid=(m // tt,),
        in_specs=[idx_spec, idx_spec,
                  pl.BlockSpec(memory_space=pl.ANY),
                  pl.BlockSpec((tt, LANES), lambda i: (i, 0)),
                  pl.BlockSpec((tt, d), lambda i: (i, 0)),
                  pl.BlockSpec((1, d), lambda i: (0, 0)),
                  pl.BlockSpec((1, d), lambda i: (0, 0))],
        out_specs=pl.BlockSpec((tt, d), lambda i: (i, 0)),
        out_shape=jax.ShapeDtypeStruct((m, d), F32),
        scratch_shapes=[pltpu.VMEM((2, tt, d), F32), pltpu.SemaphoreType.DMA((2,))],
        compiler_params=_cparams("arbitrary"),
    )(pos1.reshape(m // tt, 1, tt), pos2.reshape(m // tt, 1, tt), ys, sel, res, g.reshape(1, d), b.reshape(1, d))


def _moe_plan(sel, tm):
    m = sel.shape[0]
    i1 = sel[:, 0].astype(jnp.int32)
    i2 = sel[:, 1].astype(jnp.int32)
    e = jnp.concatenate([i1, i2])
    onehot = (e[:, None] == jnp.arange(N_EXPERTS, dtype=jnp.int32)[None, :]).astype(jnp.int32)
    csum = jnp.cumsum(onehot, axis=0)
    rank = jnp.sum(csum * onehot, axis=1) - 1
    counts = csum[-1]
    padded = ((counts + tm - 1) // tm) * tm
    ends = jnp.cumsum(padded)
    starts = ends - padded
    pos = jnp.sum(starts[None, :] * onehot, axis=1) + rank
    n_slots = 2 * m + N_EXPERTS * tm
    tok = jnp.arange(m, dtype=jnp.int32)
    src = jnp.zeros((n_slots,), jnp.int32).at[pos].set(jnp.concatenate([tok, tok]))
    tile_start = jnp.arange(n_slots // tm, dtype=jnp.int32) * tm
    tile_e = jnp.sum((tile_start[:, None] >= ends[None, :]).astype(jnp.int32), axis=1)
    n_active = (ends[-1] // tm).astype(jnp.int32).reshape(1)
    return src, pos[:m], pos[m:], jnp.minimum(tile_e, N_EXPERTS - 1), n_active


def _moe(x_a, x_b, p, g, b):
    tm = 512
    ma = x_a.shape[0]
    x = jnp.concatenate([x_a, x_b], axis=0)
    sel = _router(x, p["router_w"], p["router_b"])
    src, pos1, pos2, tile_e, n_active = _moe_plan(sel, tm)
    xg = _moe_gather(x, src, tm)
    hdn = _moe_up(xg, p["moe_w_gate"], p["moe_w_up"], tile_e, n_active, tm)
    ys = _moe_down(hdn, p["moe_w_down"], tile_e, n_active, tm)
    return (_moe_combine(ys, pos1[:ma], pos2[:ma], sel[:ma], x_a, g, b),
            _moe_combine(ys, pos1[ma:], pos2[ma:], sel[ma:], x_b, g, b))


def _gla_kernel(q_ref, k_ref, v_ref, la_ref, r_ref, s0_ref, ng_ref, o_ref, sout_ref, st_ref, bc_ref,
                *, tt, nt):
    sub = GLA_SUB
    i = pl.program_id(2)

    @pl.when(i == 0)
    def _():
        st_ref[...] = s0_ref[...].T

    rows = lax.broadcasted_iota(jnp.int32, (tt, tt), 0)
    cols = lax.broadcasted_iota(jnp.int32, (tt, tt), 1)
    tri = jnp.where(((rows & -sub) == (cols & -sub)) & (cols <= rows), 1.0, 0.0).astype(F32)
    bc_ref[...] = jnp.dot(tri, la_ref[...], precision=lax.Precision.HIGHEST, preferred_element_type=F32)
    trow = lax.broadcasted_iota(jnp.int32, (sub, GLA_DK), 0)
    lane = lax.broadcasted_iota(jnp.int32, (sub, LANES), 1)
    ng = ng_ref[...]

    rsl = [slice(c * sub, (c + 1) * sub) for c in range(tt // sub)]
    b = [bc_ref[r, :] for r in rsl]
    q = [q_ref[r, :] * (GLA_DK ** -0.5) for r in rsl]
    k = [k_ref[r, :] for r in rsl]
    att = [jnp.zeros((sub, LANES), F32) for _ in rsl]
    for s in range(sub):
        for c in range(len(rsl)):
            e = jnp.exp(jnp.where(trow >= s, b[c] - b[c][s:s + 1, :], -jnp.inf))
            col = jnp.sum(q[c] * k[c][s:s + 1, :] * e, axis=1, keepdims=True)
            att[c] = jnp.where(lane == s, col, att[c])
    intra = [_dot(att[c][:, :sub], v_ref[r, :]) for c, r in enumerate(rsl)]
    b_last = [bb[sub - 1:sub, :] for bb in b]
    upd = [_dot_tn(v_ref[r, :], k[c] * jnp.exp(b_last[c] - b[c])) for c, r in enumerate(rsl)]

    st = st_ref[...]
    for c, r in enumerate(rsl):
        o = intra[c] + _dot_nt(q[c] * jnp.exp(b[c]), st)
        st = st * jnp.exp(b_last[c]) + upd[c]
        ms = jnp.mean(o * o, axis=1, keepdims=True)
        o_ref[r, :] = (o * lax.rsqrt(ms + LN_EPS) * ng * _silu(r_ref[r, :])).astype(o_ref.dtype)
    st_ref[...] = st

    @pl.when(i == nt - 1)
    def _():
        sout_ref[...] = st_ref[...].T


def _gla(qkvr, log_a, s0, norm_g, bsz, t):
    tt = _pick_tile(t, (128, 16))
    nt = t // tt
    h = GLA_HEADS
    vb = GLA_QK * 2 // GLA_DV
    rb = vb + GLA_V // GLA_DV
    row = lambda b, hh, i: b * nt + i
    return pl.pallas_call(
        functools.partial(_gla_kernel, tt=tt, nt=nt),
        grid=(bsz, h, nt),
        in_specs=[pl.BlockSpec((tt, GLA_DK), lambda b, hh, i: (row(b, hh, i), hh)),
                  pl.BlockSpec((tt, GLA_DK), lambda b, hh, i: (row(b, hh, i), h + hh)),
                  pl.BlockSpec((tt, GLA_DV), lambda b, hh, i: (row(b, hh, i), vb + hh)),
                  pl.BlockSpec((tt, GLA_DK), lambda b, hh, i: (row(b, hh, i), hh)),
                  pl.BlockSpec((tt, GLA_DV), lambda b, hh, i: (row(b, hh, i), rb + hh)),
                  pl.BlockSpec((None, None, GLA_DK, GLA_DV), lambda b, hh, i: (b, hh, 0, 0)),
                  pl.BlockSpec((1, GLA_DV), lambda b, hh, i: (0, 0))],
        out_specs=[pl.BlockSpec((tt, GLA_DV), lambda b, hh, i: (row(b, hh, i), hh)),
                   pl.BlockSpec((None, None, GLA_DK, GLA_DV), lambda b, hh, i: (b, hh, 0, 0))],
        out_shape=[jax.ShapeDtypeStruct((bsz * t, GLA_V), BF16),
                   jax.ShapeDtypeStruct((bsz, h, GLA_DK, GLA_DV), F32)],
        scratch_shapes=[pltpu.VMEM((GLA_DV, GLA_DK), F32), pltpu.VMEM((tt, GLA_DK), F32)],
        compiler_params=_cparams("parallel", "parallel", "arbitrary"),
    )(qkvr, qkvr, qkvr, log_a, qkvr, s0, norm_g.reshape(1, GLA_DV).astype(F32))


def _conf_conv_kernel(u_ref, buf_ref, w_ref, dwb_ref, g_ref, b_ref, c_ref, bufo_ref, ext_ref, *, tt, nt):
    hist = CONV_W - 1
    pad = 32 - hist
    i = pl.program_id(1)

    @pl.when(i == 0)
    def _():
        ext_ref[0:pad, :] = jnp.zeros((pad, CONV_CH), F32)
        ext_ref[pad:32, :] = buf_ref[...]

    ext_ref[32:32 + tt, :] = u_ref[...]
    cw = 256
    for cb in range(CONV_CH // cw):
        cs = slice(cb * cw, (cb + 1) * cw)
        acc = jnp.zeros((tt, cw), F32)
        for j in range(CONV_W):
            acc = acc + w_ref[j:j + 1, cs] * ext_ref[pad + j:pad + j + tt, cs]
        c_ref[:, cs] = (acc + dwb_ref[:, cs]).astype(c_ref.dtype)
    y = _layer_norm(c_ref[...].astype(F32), g_ref[...], b_ref[...])
    c_ref[...] = _silu(y).astype(c_ref.dtype)

    @pl.when(i == nt - 1)
    def _():
        bufo_ref[...] = ext_ref[tt + pad:tt + 32, :]

    ext_ref[0:32, :] = ext_ref[tt:tt + 32, :]


def _conf_conv(u, buf, dw_w, dw_b, ln_g, ln_b, bsz, t):
    tt = _pick_tile(t, (128, 16))
    nt = t // tt
    vec = lambda a: a.reshape(1, CONV_CH).astype(F32)
    return pl.pallas_call(
        functools.partial(_conf_conv_kernel, tt=tt, nt=nt),
        grid=(bsz, nt),
        in_specs=[pl.BlockSpec((tt, CONV_CH), lambda b, i: (b * nt + i, 0)),
                  pl.BlockSpec((None, CONV_W - 1, CONV_CH), lambda b, i: (b, 0, 0)),
                  pl.BlockSpec((CONV_W, CONV_CH), lambda b, i: (0, 0)),
                  pl.BlockSpec((1, CONV_CH), lambda b, i: (0, 0)),
                  pl.BlockSpec((1, CONV_CH), lambda b, i: (0, 0)),
                  pl.BlockSpec((1, CONV_CH), lambda b, i: (0, 0))],
        out_specs=[pl.BlockSpec((tt, CONV_CH), lambda b, i: (b * nt + i, 0)),
                   pl.BlockSpec((None, CONV_W - 1, CONV_CH), lambda b, i: (b, 0, 0))],
        out_shape=[jax.ShapeDtypeStruct((bsz * t, CONV_CH), F32),
                   jax.ShapeDtypeStruct((bsz, CONV_W - 1, CONV_CH), F32)],
        scratch_shapes=[pltpu.VMEM((32 + tt, CONV_CH), F32)],
        compiler_params=_cparams("parallel", "arbitrary"),
    )(u, buf, dw_w.astype(F32), vec(dw_b), vec(ln_g), vec(ln_b))


def _gdn_conv_kernel(x_ref, buf_ref, w_ref, o_ref, bufo_ref, ext_ref, *, tt, nt, tc):
    hist = GDN_CONV_W - 1
    pad = 8 - hist
    cb = pl.program_id(1)
    i = pl.program_id(2)

    @pl.when(i == 0)
    def _():
        ext_ref[0:pad, :] = jnp.zeros((pad, tc), F32)
        ext_ref[pad:8, :] = buf_ref[...]

    ext_ref[8:8 + tt, :] = x_ref[...]
    acc = jnp.zeros((tt, tc), F32)
    for j in range(GDN_CONV_W):
        acc = acc + w_ref[j:j + 1, :] * ext_ref[pad + j:pad + j + tt, :]
    y = _silu(acc)
    is_qk = cb < (2 * GDN_Q) // tc
    scale = jnp.where(cb < GDN_Q // tc, GDN_DK ** -0.5, 1.0)
    for hh in range(tc // GDN_DK):
        cs = slice(hh * GDN_DK, (hh + 1) * GDN_DK)
        seg = y[:, cs]
        nrm = seg * lax.rsqrt(jnp.sum(seg * seg, axis=1, keepdims=True) + L2_EPS) * scale
        o_ref[:, cs] = jnp.where(is_qk, nrm, seg)

    @pl.when(i == nt - 1)
    def _():
        bufo_ref[...] = ext_ref[tt + pad:tt + 8, :]

    ext_ref[0:8, :] = ext_ref[tt:tt + 8, :]


def _gdn_conv(proj, buf, conv_w, bsz, t):
    tt = _pick_tile(t, (256, 16))
    nt = t // tt
    tc = 1024
    return pl.pallas_call(
        functools.partial(_gdn_conv_kernel, tt=tt, nt=nt, tc=tc),
        grid=(bsz, GDN_CONV_DIM // tc, nt),
        in_specs=[pl.BlockSpec((tt, tc), lambda b, c, i: (b * nt + i, c)),
                  pl.BlockSpec((None, GDN_CONV_W - 1, tc), lambda b, c, i: (b, 0, c)),
                  pl.BlockSpec((GDN_CONV_W, tc), lambda b, c, i: (0, c))],
        out_specs=[pl.BlockSpec((tt, tc), lambda b, c, i: (b * nt + i, c)),
                   pl.BlockSpec((None, GDN_CONV_W - 1, tc), lambda b, c, i: (b, 0, c))],
        out_shape=[jax.ShapeDtypeStruct((bsz * t, GDN_CONV_DIM), F32),
                   jax.ShapeDtypeStruct((bsz, GDN_CONV_W - 1, GDN_CONV_DIM), F32)],
        scratch_shapes=[pltpu.VMEM((8 + tt, tc), F32)],
        compiler_params=_cparams("parallel", "parallel", "arbitrary"),
    )(proj, buf, conv_w.astype(F32))


def _unit_lower_inverses(ms, size):
    rows = lax.broadcasted_iota(jnp.int32, (size, size), 0)
    cols = lax.broadcasted_iota(jnp.int32, (size, size), 1)
    eye = jnp.where(rows == cols, 1.0, 0.0)
    accs = [eye - m for m in ms]
    powers = [_dot(m, m) for m in ms]
    terms = 2
    while 2 * terms < size:
        boths = [_dot(jnp.concatenate([a, p], axis=0), p) for a, p in zip(accs, powers)]
        accs = [a + bt[:size] for a, bt in zip(accs, boths)]
        powers = [bt[size:] for bt in boths]
        terms *= 2
    if terms < size:
        accs = [a + _dot(a, p) for a, p in zip(accs, powers)]
    return accs


GDN_HQ_PER_STEP = 2


def _gdn_kernel(q_ref, k_ref, v_ref, z_ref, gb_ref, s0_ref, ng_ref, o_ref, sout_ref, s_ref,
                *, chunk, nchunk, nt):
    nhq = GDN_HQ_PER_STEP
    rep = GDN_V_HEADS // GDN_QK_HEADS
    nhv = nhq * rep
    hq0 = pl.program_id(1) * nhq
    i = pl.program_id(2)

    @pl.when(i == 0)
    def _():
        s_ref[...] = s0_ref[...]

    lane = lax.broadcasted_iota(jnp.int32, (chunk, LANES), 1)
    rows = lax.broadcasted_iota(jnp.int32, (chunk, chunk), 0)
    cols = lax.broadcasted_iota(jnp.int32, (chunk, chunk), 1)
    incl = cols <= rows
    strict = cols < rows
    ng = ng_ref[...]
    rsl = [slice(c * chunk, (c + 1) * chunk) for c in range(nchunk)]
    vsl = [slice(h * GDN_DV, (h + 1) * GDN_DV) for h in range(nhv)]
    pairs = [(c, a) for c in range(nchunk) for a in range(nhq)]
    chains = [(c, a, j) for c, a in pairs for j in range(rep)]

    qs = {(c, a): q_ref[rsl[c], a * GDN_DK:(a + 1) * GDN_DK] for c, a in pairs}
    ks = {(c, a): k_ref[rsl[c], a * GDN_DK:(a + 1) * GDN_DK] for c, a in pairs}
    kq = {ca: _dot_nt(jnp.concatenate([ks[ca], qs[ca]], axis=0), ks[ca]) for ca in pairs}
    beta, gcum, rel = {}, {}, {}
    for c, a, j in chains:
        gb = gb_ref[rsl[c], :]
        hv = (hq0 + a) * rep + j
        beta[c, a, j] = jnp.sum(jnp.where(lane == hv, gb, 0.0), axis=1, keepdims=True)
        g = jnp.sum(jnp.where(lane == GDN_V_HEADS + hv, gb, 0.0), axis=1, keepdims=True)
        g_row = jnp.sum(jnp.where(rows == cols, g, 0.0), axis=0, keepdims=True)
        gcum[c, a, j] = jnp.sum(jnp.where(incl, g_row, 0.0), axis=1, keepdims=True)
        gcum_row = jnp.sum(jnp.where(rows <= cols, g, 0.0), axis=0, keepdims=True)
        rel[c, a, j] = jnp.exp(jnp.where(incl, gcum[c, a, j] - gcum_row, -jnp.inf))
    tinvs = _unit_lower_inverses(
        [jnp.where(strict, kq[c, a][:chunk] * beta[c, a, j] * rel[c, a, j], 0.0) for c, a, j in chains], chunk)
    uw = {}
    for (c, a, j), tinv in zip(chains, tinvs):
        eg = jnp.exp(gcum[c, a, j])
        v = v_ref[rsl[c], vsl[a * rep + j]]
        uw[c, a, j] = _dot(tinv, jnp.concatenate([v * beta[c, a, j], ks[c, a] * (beta[c, a, j] * eg)], axis=1))

    heads = [(a, j) for a in range(nhq) for j in range(rep)]
    for c in range(nchunk):
        s = [s_ref[h] for h in range(nhv)]
        ws = []
        for h, (a, j) in enumerate(heads):
            wq = jnp.concatenate([uw[c, a, j][:, GDN_DV:], qs[c, a] * jnp.exp(gcum[c, a, j])], axis=0)
            ws.append(_dot(wq, s[h]))
        v_new = [uw[c, a, j][:, :GDN_DV] - ws[h][:chunk] for h, (a, j) in enumerate(heads)]
        outs = [ws[h][chunk:] + _dot(kq[c, a][chunk:] * rel[c, a, j], v_new[h]) for h, (a, j) in enumerate(heads)]
        for h, (a, j) in enumerate(heads):
            g_last = gcum[c, a, j][chunk - 1:chunk, :]
            kd = ks[c, a] * jnp.exp(g_last - gcum[c, a, j])
            s_ref[h] = s[h] * jnp.exp(g_last) + _dot_tn(kd, v_new[h])
        for h in range(nhv):
            o = outs[h]
            ms = jnp.mean(o * o, axis=1, keepdims=True)
            o_ref[rsl[c], vsl[h]] = (o * lax.rsqrt(ms + LN_EPS) * ng * _silu(z_ref[rsl[c], vsl[h]])).astype(o_ref.dtype)

    @pl.when(i == nt - 1)
    def _():
        sout_ref[...] = s_ref[...]


def _gdn(qkv, proj, gb, s0, norm_g, bsz, t):
    chunk = min(CHUNK, t)
    tt = _pick_tile(t, (4 * chunk, chunk))
    nt = t // tt
    nhq = GDN_HQ_PER_STEP
    nhv = nhq * (GDN_V_HEADS // GDN_QK_HEADS)
    qw = nhq * GDN_DK
    vw = nhv * GDN_DV
    row = lambda b, i: b * nt + i
    return pl.pallas_call(
        functools.partial(_gdn_kernel, chunk=chunk, nchunk=tt // chunk, nt=nt),
        grid=(bsz, GDN_QK_HEADS // nhq, nt),
        in_specs=[pl.BlockSpec((tt, qw), lambda b, h, i: (row(b, i), h)),
                  pl.BlockSpec((tt, qw), lambda b, h, i: (row(b, i), GDN_Q // qw + h)),
                  pl.BlockSpec((tt, vw), lambda b, h, i: (row(b, i), 2 * GDN_Q // vw + h)),
                  pl.BlockSpec((tt, vw), lambda b, h, i: (row(b, i), GDN_CONV_DIM // vw + h)),
                  pl.BlockSpec((tt, LANES), lambda b, h, i: (row(b, i), 0)),
                  pl.BlockSpec((None, nhv, GDN_DK, GDN_DV), lambda b, h, i: (b, h, 0, 0)),
                  pl.BlockSpec((1, GDN_DV), lambda b, h, i: (0, 0))],
        out_specs=[pl.BlockSpec((tt, vw), lambda b, h, i: (row(b, i), h)),
                   pl.BlockSpec((None, nhv, GDN_DK, GDN_DV), lambda b, h, i: (b, h, 0, 0))],
        out_shape=[jax.ShapeDtypeStruct((bsz * t, GDN_V), BF16),
                   jax.ShapeDtypeStruct((bsz, GDN_V_HEADS, GDN_DK, GDN_DV), F32)],
        scratch_shapes=[pltpu.VMEM((nhv, GDN_DK, GDN_DV), F32)],
        compiler_params=_cparams("parallel", "parallel", "arbitrary"),
    )(qkv, qkv, qkv, proj, gb, s0, norm_g.reshape(1, GDN_DV).astype(F32))


def _mem_attn_kernel(q_ref, k_ref, v_ref, o_ref, kh_ref, vh_ref):
    @pl.when(pl.program_id(1) == 0)
    def _():
        for h in range(MEM_HEADS):
            kh_ref[h] = k_ref[:, h, :].astype(BF16)
            vh_ref[h] = v_ref[:, h, :].astype(BF16)

    for h in range(MEM_HEADS):
        hs = slice(h * MEM_HD, (h + 1) * MEM_HD)
        s = _dot_nt(q_ref[:, hs], kh_ref[h]) * (MEM_HD ** -0.5)
        s = s - jnp.max(s, axis=1, keepdims=True)
        p = jnp.exp(s)
        p = p / jnp.sum(p, axis=1, keepdims=True)
        o_ref[:, hs] = _dot(p, vh_ref[h]).astype(o_ref.dtype)


def _mem_attn(q, mk, mv, layer, bsz, t):
    tq = _pick_tile(t, (512, 16))
    nt = t // tq
    kv_spec = pl.BlockSpec((None, None, MEM_LEN, MEM_HEADS, MEM_HD), lambda b, i: (layer, b, 0, 0, 0))
    return pl.pallas_call(
        _mem_attn_kernel,
        grid=(bsz, nt),
        in_specs=[pl.BlockSpec((tq, D_MODEL), lambda b, i: (b * nt + i, 0)), kv_spec, kv_spec],
        out_specs=pl.BlockSpec((tq, D_MODEL), lambda b, i: (b * nt + i, 0)),
        out_shape=jax.ShapeDtypeStruct((bsz * t, D_MODEL), BF16),
        scratch_shapes=[pltpu.VMEM((MEM_HEADS, MEM_LEN, MEM_HD), BF16),
                        pltpu.VMEM((MEM_HEADS, MEM_LEN, MEM_HD), BF16)],
        compiler_params=_cparams("parallel", "arbitrary"),
    )(q, mk, mv)


def _trunk(x, bsz, t, gla_s, conf_buf, gdn_s, gdn_buf, mem_k, mem_v, p):
    m = bsz * t
    xb = x.astype(BF16)

    qkvr = _matmul(xb, p["w_qkvr"], F32)[0]
    log_a = _gla_gate(xb, p["w_gl"], p["w_gk_up"], p["b_gk"])
    u = _mm_gated(xb, p["w_glu"], p["w_glu"], CONV_CH, CONV_CH, "glu", F32)
    o_a, gla_new = _gla(qkvr, log_a, gla_s, p["gla_norm_g"], bsz, t)
    c, conf_new = _conf_conv(u, conf_buf, p["conv_dw_w"], p["conv_dw_b"], p["conv_ln_g"], p["conv_ln_b"], bsz, t)
    mix = jnp.concatenate([o_a, c.astype(BF16)], axis=1)
    x, xb = _mm_res_ln(mix, p["w_out_ab"], x, p["ln_mix_g"][0], p["ln_mix_b"][0])
    qm = _matmul(xb, p["w_mq"][0:1], BF16)[0]
    att = _mem_attn(qm, mem_k, mem_v, 0, bsz, t)
    x, xb = _mm_res_ln(att, p["w_mo"][0], x, p["ln_mem_g"][0], p["ln_mem_b"][0])
    hdn = _mm_gated(xb, p["ff_w_gate"], p["ff_w_up"], p["ff_w_gate"].shape[2], 0, "swiglu", BF16)
    x, xb = _mm_res_ln(hdn, p["ff_w_down"], x, p["ln_ff_g"][0], p["ln_ff_b"][0])

    proj = _matmul(xb, p["w_qkvz"], F32)[0]
    gb = _gdn_gate(xb, p["w_ba"], p["gdn_a_log"], p["gdn_dt_bias"])
    qkv, gdnc_new = _gdn_conv(proj, gdn_buf, p["gdn_conv_w"], bsz, t)
    o_c, gdn_new = _gdn(qkv, proj, gb, gdn_s, p["gdn_norm_g"], bsz, t)
    x, xb = _mm_res_ln(o_c, p["w_out_c"], x, p["ln_mix_g"][1], p["ln_mix_b"][1])
    qm = _matmul(xb, p["w_mq"][1:2], BF16)[0]
    att = _mem_attn(qm, mem_k, mem_v, 1, bsz, t)
    x, xb = _mm_res_ln(att, p["w_mo"][1], x, p["ln_mem_g"][1], p["ln_mem_b"][1])
    return x, gla_new, conf_new, gdn_new, gdnc_new


def kernel(x_prompt, x_sample, mem_prompt, state_gla, state_conf_conv, state_gdn, state_gdn_conv, cache_mem_k, cache_mem_v, ln_mix_g, ln_mix_b, ln_mem_g, ln_mem_b, ln_ff_g, ln_ff_b, w_mq, w_mk, w_mv, w_mo, w_in_ab, w_gk_up, b_gk, gla_norm_g, conv_dw_w, conv_dw_b, conv_ln_g, conv_ln_b, w_out_ab, ff_w_gate, ff_w_up, ff_w_down, w_in_c, gdn_conv_w, gdn_a_log, gdn_dt_bias, gdn_norm_g, w_out_c, router_w, router_b, moe_w_gate, moe_w_up, moe_w_down):
    bf = lambda a: a.astype(BF16)
    qkvr_n = 2 * GLA_QK + 2 * GLA_V
    glu_0 = qkvr_n + GLA_GATE_RANK
    qkvz_n = GDN_CONV_DIM + GDN_V
    p = dict(
        ln_mix_g=ln_mix_g, ln_mix_b=ln_mix_b, ln_mem_g=ln_mem_g, ln_mem_b=ln_mem_b, ln_ff_g=ln_ff_g, ln_ff_b=ln_ff_b,
        w_mq=bf(w_mq), w_mo=bf(w_mo),
        w_qkvr=bf(w_in_ab[:, :, :qkvr_n]), w_gl=w_in_ab[0, :, qkvr_n:glu_0], w_glu=bf(w_in_ab[:, :, glu_0:]),
        w_gk_up=w_gk_up[0], b_gk=b_gk[0], gla_norm_g=gla_norm_g[0],
        conv_dw_w=conv_dw_w[0], conv_dw_b=conv_dw_b[0], conv_ln_g=conv_ln_g[0], conv_ln_b=conv_ln_b[0],
        w_out_ab=bf(w_out_ab[0]), ff_w_gate=bf(ff_w_gate), ff_w_up=bf(ff_w_up), ff_w_down=bf(ff_w_down[0]),
        w_qkvz=bf(w_in_c[:, :, :qkvz_n]), w_ba=w_in_c[0, :, qkvz_n:], gdn_conv_w=gdn_conv_w[0],
        gdn_a_log=gdn_a_log[0], gdn_dt_bias=gdn_dt_bias[0], gdn_norm_g=gdn_norm_g[0], w_out_c=bf(w_out_c[0]),
        router_w=router_w[0], router_b=router_b[0],
        moe_w_gate=moe_w_gate[0], moe_w_up=moe_w_up[0], moe_w_down=moe_w_down[0],
    )
    nb, nt, d = x_prompt.shape
    sb, st, _ = x_sample.shape
    dt = x_prompt.dtype

    mem = bf(mem_prompt.reshape(nb * MEM_LEN, d))
    kv_shape = (DEPTH, nb, MEM_LEN, MEM_HEADS, MEM_HD)
    mem_k_p = _matmul(mem, bf(w_mk), F32).reshape(kv_shape)
    mem_v_p = _matmul(mem, bf(w_mv), F32).reshape(kv_shape)
    y_p, gla_p, conf_p, gdn_p, gdnc_p = _trunk(
        x_prompt.reshape(nb * nt, d), nb, nt,
        jnp.zeros((nb, GLA_HEADS, GLA_DK, GLA_DV), dt), jnp.zeros((nb, CONV_W - 1, CONV_CH), dt),
        jnp.zeros((nb, GDN_V_HEADS, GDN_DK, GDN_DV), dt), jnp.zeros((nb, GDN_CONV_W - 1, GDN_CONV_DIM), dt),
        mem_k_p, mem_v_p, p)

    y_s, gla_s, conf_s, gdn_s, gdnc_s = _trunk(
        x_sample.reshape(sb * st, d), sb, st, state_gla[0], state_conf_conv[0], state_gdn[0], state_gdn_conv[0],
        cache_mem_k, cache_mem_v, p)

    y_p, y_s = _moe(y_p, y_s, p, ln_ff_g[1], ln_ff_b[1])

    return (y_p.reshape(nb, nt, d), y_s.reshape(sb, st, d),
            gla_p[None], conf_p[None], gdn_p[None], gdnc_p[None], mem_k_p, mem_v_p,
            gla_s[None], conf_s[None], gdn_s[None], gdnc_s[None])
```
